```python
import math, functools
import jax, jax.numpy as jnp
from jax import lax
import numpy as np

D_MODEL = 1024
BATCH = 1
SEQ = 16384
DEPTH = 2
DEC_BATCH = 32
DEC_SEQ = 8
PAST_LEN = 16384
PAGE_SIZE = 128

H_SB = 8
D_SB = 64
W_SB = H_SB * D_SB
Q_BLOCK = 128
SB_BIAS_INIT = -6.0
SP_GROUPS = 8
W_SP = 512
CHUNK = 128
N_MEM = 256
H_MEM = 4
D_MEM = 128
W_MEM = H_MEM * D_MEM
N_BRANCH = 3
W_BR = 512
D_FF = 2816
CONV_W = 3
EPS = 1e-6
D_IN = 3 * W_SB + 2 * W_SP + W_MEM + N_BRANCH * D_MODEL

kernel_name = 'hybrid_stickbreak_gmlp_memxattn_decoder_step'


def rmsnorm(x, g):
    xf = x.astype(jnp.float32)
    y = xf * lax.rsqrt(jnp.mean(xf * xf, axis=-1, keepdims=True) + EPS)
    return (y * g.astype(jnp.float32)).astype(x.dtype)


def split_in(z):
    sizes = [W_SB, W_SB, W_SB, W_SP, W_SP, W_MEM, N_BRANCH * D_MODEL]
    return jnp.split(z, [int(c) for c in np.cumsum(sizes)[:-1]], axis=-1)


def stick_breaking(q, k, v, bias, q_pos, k_pos):
    z = jnp.einsum('bqhd,bkhd->bhqk', q, k).astype(jnp.float32) * (D_SB ** -0.5)
    z = z + bias.astype(jnp.float32)[None, :, None, None]
    mask = k_pos[None, :] < q_pos[:, None]
    log_not = jnp.where(mask, jax.nn.log_sigmoid(-z), 0.0)
    later = lax.cumsum(log_not, axis=3, reverse=True) - log_not
    a = jnp.where(mask, jnp.exp(jax.nn.log_sigmoid(z) + later), 0.0)
    return jnp.einsum('bhqk,bkhd->bqhd', a.astype(v.dtype), v)


def sb_prompt(q, k, v, bias):
    B, T, H, D = q.shape
    nb = T // Q_BLOCK
    qb = jnp.moveaxis(q.reshape(B, nb, Q_BLOCK, H, D), 1, 0)
    starts = jnp.arange(nb, dtype=jnp.int32) * Q_BLOCK
    k_pos = jnp.arange(T, dtype=jnp.int32)
    offs = jnp.arange(Q_BLOCK, dtype=jnp.int32)

    def one_block(args):
        qi, s0 = args
        return stick_breaking(qi, k, v, bias, s0 + offs, k_pos)

    o = lax.map(one_block, (qb, starts))
    return jnp.moveaxis(o, 0, 1).reshape(B, T, H, D)


def sb_sample(q, k, v, bias, past_k, past_v):
    T = q.shape[1]
    P = past_k.shape[1]
    kk = jnp.concatenate([past_k.astype(k.dtype), k], axis=1)
    vv = jnp.concatenate([past_v.astype(v.dtype), v], axis=1)
    q_pos = P + jnp.arange(T, dtype=jnp.int32)
    k_pos = jnp.arange(P + T, dtype=jnp.int32)
    return stick_breaking(q, kk, vv, bias, q_pos, k_pos)


def gather_pages(pool, page_table):
    g = pool[page_table]
    return g.reshape(g.shape[0], g.shape[1] * g.shape[2], g.shape[3], g.shape[4])


def spatial_gate(u, v, sp_norm_l, sp_w_l, sp_b_l):
    B, T, _ = v.shape
    L = min(T, CHUNK)
    nc = T // L
    vn = rmsnorm(v, sp_norm_l)
    vc = vn.reshape(B, nc, L, SP_GROUPS, W_SP // SP_GROUPS)
    w = jnp.tril(sp_w_l[:, :L, :L])
    s = jnp.einsum('gts,bnsgc->bntgc', w, vc) + sp_b_l[:, :L].T[None, None, :, :, None]
    return u * s.reshape(B, T, W_SP).astype(u.dtype), vn


def mem_kv(mem, mem_norm_l, w_mem_kv_l):
    B, M, _ = mem.shape
    kv = rmsnorm(mem, mem_norm_l) @ w_mem_kv_l
    mk, mv = jnp.split(kv, 2, axis=-1)
    return mk.reshape(B, M, H_MEM, D_MEM), mv.reshape(B, M, H_MEM, D_MEM)


def mem_attend(q, mk, mv):
    s = jnp.einsum('bthd,bmhd->bhtm', q, mk.astype(q.dtype)).astype(jnp.float32) * (D_MEM ** -0.5)
    p = jax.nn.softmax(s, axis=-1)
    return jnp.einsum('bhtm,bmhd->bthd', p.astype(q.dtype), mv.astype(q.dtype))


def token_mix(h, norm_g, w_in_l, sb_bias_l, sp_norm_l, sp_w_l, sp_b_l, w_branch_l, w_o_l,
              sb_attend, mk, mv):
    B, T, _ = h.shape
    a = rmsnorm(h, norm_g)
    q_sb, k_sb, v_sb, u_sp, v_sp, q_mem, g = split_in(a @ w_in_l)
    q_sb = q_sb.reshape(B, T, H_SB, D_SB)
    k_sb = k_sb.reshape(B, T, H_SB, D_SB)
    v_sb = v_sb.reshape(B, T, H_SB, D_SB)
    o_sb = sb_attend(q_sb, k_sb, v_sb, sb_bias_l).reshape(B, T, W_SB)
    o_sp, vn = spatial_gate(jax.nn.gelu(u_sp), jax.nn.gelu(v_sp), sp_norm_l, sp_w_l, sp_b_l)
    o_mem = mem_attend(q_mem.reshape(B, T, H_MEM, D_MEM), mk, mv).reshape(B, T, W_MEM)
    gates = jax.nn.sigmoid(g.reshape(B, T, N_BRANCH, D_MODEL))
    br = jnp.einsum('btiw,iwd->btid', jnp.stack([o_sb, o_sp, o_mem], axis=2), w_branch_l)
    mixed = jnp.einsum('btid,btid->btd', gates, br) @ w_o_l
    return h + mixed, k_sb, v_sb, vn


def conv_ffn(h, prev, norm_g, w_up_l, conv_w_l, conv_b_l, w_down_l):
    f = rmsnorm(h, norm_g)
    up = f @ w_up_l
    T = up.shape[1]
    xp = jnp.concatenate([prev.astype(up.dtype), up], axis=1)
    c = conv_b_l + sum(conv_w_l[i] * xp[:, i:i + T] for i in range(CONV_W))
    ga, gb = jnp.split(c, 2, axis=-1)
    y = (jax.nn.gelu(ga) * gb) @ w_down_l
    return h + y, xp[:, T:]


def setup_inputs(seed: int = 0) -> dict:
    key = jax.random.key(seed)
    ks = jax.random.split(key, 25)
    n_pages = PAST_LEN // PAGE_SIZE
    n_pool = (5 * DEC_BATCH * n_pages) // 4

    def nrm(k, shape, scale=1.0):
        return jax.random.normal(k, shape, jnp.float32) * scale

    def gain(k, shape):
        return 1.0 + 0.02 * jax.random.normal(k, shape, jnp.float32)

    page_table = jax.random.permutation(ks[5], n_pool)[: DEC_BATCH * n_pages]
    page_table = page_table.reshape(DEC_BATCH, n_pages).astype(jnp.int32)
    return {
        'x_prompt': nrm(ks[0], (BATCH, SEQ, D_MODEL)),
        'x_sample': nrm(ks[1], (DEC_BATCH, DEC_SEQ, D_MODEL)),
        'mem_prompt': nrm(ks[2], (BATCH, N_MEM, D_MODEL)),
        'cache_sb_k': nrm(ks[3], (DEPTH, n_pool, PAGE_SIZE, H_SB, D_SB)),
        'cache_sb_v': nrm(ks[4], (DEPTH, n_pool, PAGE_SIZE, H_SB, D_SB)),
        'page_table': page_table,
        'cache_mem_k': nrm(ks[6], (DEPTH, DEC_BATCH, N_MEM, H_MEM, D_MEM)),
        'cache_mem_v': nrm(ks[7], (DEPTH, DEC_BATCH, N_MEM, H_MEM, D_MEM)),
        'state_ffn_conv': nrm(ks[8], (DEPTH, DEC_BATCH, CONV_W - 1, 2 * D_FF)),
        'norm_mix': gain(ks[9], (DEPTH, D_MODEL)),
        'w_in': nrm(ks[10], (DEPTH, D_MODEL, D_IN), D_MODEL ** -0.5),
        'sb_bias': SB_BIAS_INIT + nrm(ks[24], (DEPTH, H_SB), 0.1),
        'sp_norm': gain(ks[11], (DEPTH, W_SP)),
        'sp_w': nrm(ks[12], (DEPTH, SP_GROUPS, CHUNK, CHUNK), CHUNK ** -0.5),
        'sp_b': gain(ks[13], (DEPTH, SP_GROUPS, CHUNK)),
        'mem_norm': gain(ks[14], (DEPTH, D_MODEL)),
        'w_mem_kv': nrm(ks[15], (DEPTH, D_MODEL, 2 * W_MEM), D_MODEL ** -0.5),
        'w_branch': nrm(ks[16], (DEPTH, N_BRANCH, W_BR, D_MODEL), W_BR ** -0.5),
        'w_o': nrm(ks[17], (DEPTH, D_MODEL, D_MODEL), D_MODEL ** -0.5),
        'norm_ffn': gain(ks[18], (DEPTH, D_MODEL)),
        'w_up': nrm(ks[19], (DEPTH, D_MODEL, 2 * D_FF), D_MODEL ** -0.5),
        'conv_w': nrm(ks[20], (DEPTH, CONV_W, 2 * D_FF), CONV_W ** -0.5),
        'conv_b': nrm(ks[21], (DEPTH, 2 * D_FF), 0.02),
        'w_down': nrm(ks[22], (DEPTH, D_FF, D_MODEL), D_FF ** -0.5),
        'norm_final': gain(ks[23], (D_MODEL,)),
    }


def reference(x_prompt, x_sample, mem_prompt, cache_sb_k, cache_sb_v, page_table,
              cache_mem_k, cache_mem_v, state_ffn_conv, norm_mix, w_in, sb_bias, sp_norm, sp_w,
              sp_b, mem_norm, w_mem_kv, w_branch, w_o, norm_ffn, w_up, conv_w, conv_b, w_down,
              norm_final):
    hp = x_prompt
    hs = x_sample
    pk, pv, sk, sv, pmk, pmv, spv, pconv, sconv = [], [], [], [], [], [], [], [], []
    for l in range(DEPTH):
        mk, mv = mem_kv(mem_prompt, mem_norm[l], w_mem_kv[l])
        hp, k_new, v_new, _ = token_mix(hp, norm_mix[l], w_in[l], sb_bias[l], sp_norm[l], sp_w[l],
                                        sp_b[l], w_branch[l], w_o[l], sb_prompt, mk, mv)
        prev0 = jnp.zeros((hp.shape[0], CONV_W - 1, 2 * D_FF), hp.dtype)
        hp, conv_p = conv_ffn(hp, prev0, norm_ffn[l], w_up[l], conv_w[l], conv_b[l], w_down[l])
        Bp, Tp = k_new.shape[0], k_new.shape[1]
        pk.append(k_new.reshape(Bp * Tp // PAGE_SIZE, PAGE_SIZE, H_SB, D_SB))
        pv.append(v_new.reshape(Bp * Tp // PAGE_SIZE, PAGE_SIZE, H_SB, D_SB))
        pmk.append(mk)
        pmv.append(mv)
        pconv.append(conv_p)

        past_k = gather_pages(cache_sb_k[l], page_table)
        past_v = gather_pages(cache_sb_v[l], page_table)
        attend = functools.partial(sb_sample, past_k=past_k, past_v=past_v)
        hs, k_s, v_s, vn_s = token_mix(hs, norm_mix[l], w_in[l], sb_bias[l], sp_norm[l], sp_w[l],
                                       sp_b[l], w_branch[l], w_o[l], attend,
                                       cache_mem_k[l], cache_mem_v[l])
        hs, conv_s = conv_ffn(hs, state_ffn_conv[l], norm_ffn[l], w_up[l], conv_w[l], conv_b[l], w_down[l])
        sk.append(k_s)
        sv.append(v_s)
        spv.append(vn_s)
        sconv.append(conv_s)

    y_prompt = rmsnorm(hp, norm_final)
    y_sample = rmsnorm(hs, norm_final)
    new_sb_k_prompt = jnp.stack(pk)
    new_sb_v_prompt = jnp.stack(pv)
    new_sb_k_sample = jnp.stack(sk)
    new_sb_v_sample = jnp.stack(sv)
    new_mem_k_prompt = jnp.stack(pmk)
    new_mem_v_prompt = jnp.stack(pmv)
    new_sp_v_sample = jnp.stack(spv)
    new_ffn_conv_prompt = jnp.stack(pconv)
    new_ffn_conv_sample = jnp.stack(sconv)
    return (y_prompt, y_sample, new_sb_k_prompt, new_sb_v_prompt, new_sb_k_sample, new_sb_v_sample,
            new_mem_k_prompt, new_mem_v_prompt, new_sp_v_sample, new_ffn_conv_prompt, new_ffn_conv_sample)
```

```python
import functools
import math

import jax
import jax.numpy as jnp
from jax import lax
from jax.experimental import pallas as pl
from jax.experimental.pallas import tpu as pltpu

F32 = jnp.float32
BF16 = jnp.bfloat16

D_MODEL = 1024
H_SB = 8
D_SB = 64
W_SB = H_SB * D_SB
SP_GROUPS = 8
W_SP = 512
CHUNK = 128
N_MEM = 256
H_MEM = 4
D_MEM = 128
W_MEM = H_MEM * D_MEM
N_BRANCH = 3
D_FF = 2816
CONV_W = 3
EPS = 1e-6
PAGE_SIZE = 128

LOG2E = 1.4426950408889634
LANE = 128
SUBLANE = 8
ROW_BLOCK = 256
AUG = 2 * D_SB
FFN_COLS = 256
PAGES_PER_STEP = 8
VMEM_LIMIT = 56 * 1024 * 1024

C_QA = 0
C_KA = C_QA + H_SB * AUG
C_K = C_KA + H_SB * AUG
C_V = C_K + W_SB
C_U = C_V + W_SB
C_VS = C_U + W_SP
C_QM = C_VS + W_SP
C_G = C_QM + W_MEM
C_END = C_G + N_BRANCH * D_MODEL


def _params(n_axes, limit=VMEM_LIMIT):
    return pltpu.CompilerParams(
        dimension_semantics=("arbitrary",) * n_axes, vmem_limit_bytes=limit)


def _resident(shape, index_map):
    return pl.BlockSpec(shape, index_map, pipeline_mode=pl.Buffered(1))


def _rms(x, gain):
    return x * lax.rsqrt(jnp.mean(x * x, axis=-1, keepdims=True) + EPS) * gain


def _gelu(x):
    c = math.sqrt(2.0 / math.pi)
    return x * (0.5 * (1.0 + jnp.tanh(c * (x + 0.044715 * (x * x * x)))))


def _dot(a, b):
    return jnp.dot(a, b, preferred_element_type=F32)


def _softplus2(z):
    return jnp.maximum(z, 0.0) + jnp.log2(1.0 + jnp.exp2(-jnp.abs(z)))


def _memkv_kernel(mem_ref, g_ref, w_ref, kv_ref):
    a = _rms(mem_ref[...], g_ref[...]).astype(BF16)
    kv_ref[...] = _dot(a, w_ref[...])


def _memkv(mem, mem_norm, w_mem_kv_bf):
    depth = w_mem_kv_bf.shape[0]
    return pl.pallas_call(
        _memkv_kernel,
        grid=(depth,),
        in_specs=[
            pl.BlockSpec((N_MEM, D_MODEL), lambda l: (0, 0)),
            pl.BlockSpec((None, 1, D_MODEL), lambda l: (l, 0, 0)),
            pl.BlockSpec((None, D_MODEL, 2 * W_MEM), lambda l: (l, 0, 0)),
        ],
        out_specs=pl.BlockSpec((None, N_MEM, 2 * W_MEM), lambda l: (l, 0, 0)),
        out_shape=jax.ShapeDtypeStruct((depth, N_MEM, 2 * W_MEM), F32),
        compiler_params=_params(1),
        name="memkv",
    )(mem, mem_norm.reshape(depth, 1, D_MODEL), w_mem_kv_bf)


def _inproj_kernel(x_ref, g_ref, w_ref, brow_ref, spn_ref,
                   qT_ref, ka_ref, k_ref, v_ref, vT_ref, ug_ref, vn_ref, qm_ref, gate_ref):
    a = _rms(x_ref[...], g_ref[...]).astype(BF16)

    def mm(c0, c1):
        return _dot(a, w_ref[:, c0:c1])

    qa = mm(C_QA, C_KA) + brow_ref[:, :H_SB * AUG]
    qT_ref[...] = qa.T.astype(BF16)
    ka = mm(C_KA, C_K) + brow_ref[:, H_SB * AUG:]
    ka_ref[...] = ka.astype(BF16)
    k_ref[...] = mm(C_K, C_V)
    v = mm(C_V, C_U)
    v_ref[...] = v
    vT_ref[...] = v.T.astype(BF16)
    ug_ref[...] = _gelu(mm(C_U, C_VS)).astype(BF16)
    vn_ref[...] = _rms(_gelu(mm(C_VS, C_QM)), spn_ref[...])
    qm_ref[...] = mm(C_QM, C_G)
    for i in range(N_BRANCH):
        g = mm(C_G + i * D_MODEL, C_G + (i + 1) * D_MODEL)
        gate_ref[:, i * D_MODEL:(i + 1) * D_MODEL] = jax.nn.sigmoid(g)


def _inproj(x, gain, w_all, brow, sp_norm):
    rows = x.shape[0]
    tm = ROW_BLOCK
    nb = rows // tm
    row = lambda i: (i, 0)
    const = lambda i: (0, 0)
    out_shapes = (
        jax.ShapeDtypeStruct((nb, H_SB * AUG, tm), BF16),
        jax.ShapeDtypeStruct((rows, H_SB * AUG), BF16),
        jax.ShapeDtypeStruct((rows, W_SB), F32),
        jax.ShapeDtypeStruct((rows, W_SB), F32),
        jax.ShapeDtypeStruct((nb, W_SB, tm), BF16),
        jax.ShapeDtypeStruct((rows, W_SP), BF16),
        jax.ShapeDtypeStruct((rows, W_SP), F32),
        jax.ShapeDtypeStruct((rows, W_MEM), F32),
        jax.ShapeDtypeStruct((rows, N_BRANCH * D_MODEL), F32),
    )
    out_specs = (
        pl.BlockSpec((None, H_SB * AUG, tm), lambda i: (i, 0, 0)),
        pl.BlockSpec((tm, H_SB * AUG), row),
        pl.BlockSpec((tm, W_SB), row),
        pl.BlockSpec((tm, W_SB), row),
        pl.BlockSpec((None, W_SB, tm), lambda i: (i, 0, 0)),
        pl.BlockSpec((tm, W_SP), row),
        pl.BlockSpec((tm, W_SP), row),
        pl.BlockSpec((tm, W_MEM), row),
        pl.BlockSpec((tm, N_BRANCH * D_MODEL), row),
    )
    return pl.pallas_call(
        _inproj_kernel,
        grid=(nb,),
        in_specs=[
            pl.BlockSpec((tm, D_MODEL), row),
            _resident((1, D_MODEL), const),
            _resident((D_MODEL, C_END), const),
            _resident((1, 2 * H_SB * AUG), const),
            _resident((1, W_SP), const),
        ],
        out_specs=out_specs,
        out_shape=out_shapes,
        compiler_params=_params(1),
        name="inproj",
    )(x, gain, w_all, brow, sp_norm)


def _sb_tile(z, carry, upper, vT_blk, valid):
    sp = _softplus2(z)
    log_beta = z - sp
    if valid is not None:
        sp = jnp.where(valid, sp, 0.0)
    newer = _dot(upper, sp.astype(BF16))
    a = jnp.exp2(log_beta - newer - carry)
    if valid is not None:
        a = jnp.where(valid, a, 0.0)
    o = _dot(vT_blk, a.astype(BF16))
    return o, carry + newer[0:1, :] + sp[0:1, :]


def _sbp_kernel(qT_ref, ka_ref, vT_ref, o_ref, acc_ref):
    t = ROW_BLOCK
    qi = pl.program_id(1)
    rk = lax.broadcasted_iota(jnp.int32, (t, t), 0)
    cq = lax.broadcasted_iota(jnp.int32, (t, t), 1)
    upper = jnp.where(cq > rk, 1.0, 0.0).astype(BF16)
    causal = rk < cq

    for hh in range(2):
        qT = qT_ref[hh * AUG:(hh + 1) * AUG, :]
        rows = slice(hh * D_SB, (hh + 1) * D_SB)
        lanes = slice(hh * AUG, (hh + 1) * AUG)

        r0 = pl.multiple_of(qi * t, t)
        z = _dot(ka_ref[pl.ds(r0, t), lanes], qT)
        o, carry = _sb_tile(z, jnp.zeros((1, t), F32), upper, vT_ref[qi, rows, :], causal)
        acc_ref[rows, :] = o

        def body(jj, carry):
            j = qi - 1 - jj
            rj = pl.multiple_of(j * t, t)
            z = _dot(ka_ref[pl.ds(rj, t), lanes], qT)
            o, carry = _sb_tile(z, carry, upper, vT_ref[j, rows, :], None)
            acc_ref[rows, :] += o
            return carry

        lax.fori_loop(0, qi, body, carry)

    o_ref[...] = acc_ref[...].T


def _sb_prompt(qT, ka, vT):
    nb = qT.shape[0]
    t = ROW_BLOCK
    rows = nb * t
    return pl.pallas_call(
        _sbp_kernel,
        grid=(H_SB // 2, nb),
        in_specs=[
            pl.BlockSpec((None, 2 * AUG, t), lambda hp, qi: (qi, hp, 0)),
            pl.BlockSpec((rows, 2 * AUG), lambda hp, qi: (0, hp)),
            pl.BlockSpec((nb, 2 * D_SB, t), lambda hp, qi: (0, hp, 0)),
        ],
        out_specs=pl.BlockSpec((t, 2 * D_SB), lambda hp, qi: (qi, hp)),
        out_shape=jax.ShapeDtypeStruct((rows, W_SB), F32),
        scratch_shapes=[pltpu.VMEM((2 * D_SB, t), F32)],
        compiler_params=_params(2),
        name="sb_prompt",
    )(qT, ka, vT)


def _sbs_kernel(pt_ref, qbd_ref, bias_ref, kn_ref, vn_ref, *refs, n_steps):
    pps = PAGES_PER_STEP
    k_refs = refs[:pps]
    v_refs = refs[pps:2 * pps]
    o_ref = refs[2 * pps]
    acc_ref, carry_ref = refs[2 * pps + 1:]
    del pt_ref
    p = pl.program_id(1)
    nq = H_SB * SUBLANE
    qbd = qbd_ref[...]
    bias = bias_ref[...]
    rj = lax.broadcasted_iota(jnp.int32, (PAGE_SIZE, 2 * PAGE_SIZE), 0)
    cs = lax.broadcasted_iota(jnp.int32, (PAGE_SIZE, 2 * PAGE_SIZE), 1)
    newer_tot = jnp.where((rj > cs) | (cs >= PAGE_SIZE), 1.0, 0.0).astype(BF16)

    def tile(k_blk, v_blk, valid):
        z = lax.dot_general(qbd, k_blk.astype(BF16), (((1,), (1,)), ((), ())),
                            preferred_element_type=F32) + bias
        sp = _softplus2(z)
        log_beta = z - sp
        if valid is not None:
            sp = jnp.where(valid, sp, 0.0)
        nt = _dot(sp.astype(BF16), newer_tot)
        a = jnp.exp2(log_beta - nt[:, :PAGE_SIZE] - carry_ref[...])
        if valid is not None:
            a = jnp.where(valid, a, 0.0)
        acc_ref[...] += _dot(a.astype(BF16), v_blk.astype(BF16))
        carry_ref[...] += nt[:, PAGE_SIZE:]

    @pl.when(p == 0)
    def _():
        acc_ref[...] = jnp.zeros_like(acc_ref)
        carry_ref[...] = jnp.zeros_like(carry_ref)
        pad = jnp.zeros((PAGE_SIZE - SUBLANE, W_SB), F32)
        kn = jnp.concatenate([kn_ref[...], pad], axis=0)
        vn = jnp.concatenate([vn_ref[...], pad], axis=0)
        tq = lax.broadcasted_iota(jnp.int32, (nq, PAGE_SIZE), 0) % SUBLANE
        kj = lax.broadcasted_iota(jnp.int32, (nq, PAGE_SIZE), 1)
        tile(kn, vn, kj < tq)

    for i in range(pps):
        tile(k_refs[i][...], v_refs[i][...], None)

    @pl.when(p == n_steps - 1)
    def _():
        acc = acc_ref[...]
        rh = lax.broadcasted_iota(jnp.int32, acc.shape, 0) // SUBLANE
        ch = lax.broadcasted_iota(jnp.int32, acc.shape, 1) // D_SB
        own = jnp.where(rh == ch, acc, 0.0)
        out = own[0:SUBLANE, :]
        for h in range(1, H_SB):
            out = out + own[h * SUBLANE:(h + 1) * SUBLANE, :]
        o_ref[...] = out


def _sb_sample(layer, page_table, qbd, bias_full, k_new, v_new, cache_k, cache_v):
    n_seq, n_pages = page_table.shape
    pps = PAGES_PER_STEP
    n_steps = n_pages // pps
    nq = H_SB * SUBLANE

    def page_spec(i):
        def imap(b, p, pt):
            return (layer, pt[b, n_pages - 1 - (p * pps + i)], 0, 0)
        return pl.BlockSpec((None, None, PAGE_SIZE, W_SB), imap)

    grid_spec = pltpu.PrefetchScalarGridSpec(
        num_scalar_prefetch=1,
        grid=(n_seq, n_steps),
        in_specs=[
            pl.BlockSpec((None, nq, W_SB), lambda b, p, pt: (b, 0, 0)),
            pl.BlockSpec((nq, PAGE_SIZE), lambda b, p, pt: (0, 0)),
            pl.BlockSpec((SUBLANE, W_SB), lambda b, p, pt: (b, 0)),
            pl.BlockSpec((SUBLANE, W_SB), lambda b, p, pt: (b, 0)),
        ] + [page_spec(i) for i in range(pps)] * 2,
        out_specs=pl.BlockSpec((SUBLANE, W_SB), lambda b, p, pt: (b, 0)),
        scratch_shapes=[pltpu.VMEM((nq, W_SB), F32), pltpu.VMEM((nq, PAGE_SIZE), F32)],
    )
    return pl.pallas_call(
        functools.partial(_sbs_kernel, n_steps=n_steps),
        grid_spec=grid_spec,
        out_shape=jax.ShapeDtypeStruct((n_seq * SUBLANE, W_SB), F32),
        compiler_params=_params(2),
        name="sb_sample",
    )(page_table, qbd, bias_full, k_new, v_new,
      *([cache_k] * pps), *([cache_v] * pps))


def _memattn_kernel(q_ref, mk_ref, mv_ref, o_ref):
    for h in range(H_MEM):
        cols = slice(h * D_MEM, (h + 1) * D_MEM)
        q = q_ref[:, cols].astype(BF16)
        s = lax.dot_general(q, mk_ref[:, cols].astype(BF16), (((1,), (1,)), ((), ())),
                            preferred_element_type=F32)
        e = jnp.exp(s - jnp.max(s, axis=-1, keepdims=True))
        p = e / jnp.sum(e, axis=-1, keepdims=True)
        o_ref[:, cols] = _dot(p.astype(BF16), mv_ref[:, cols].astype(BF16))


def _mem_attend(q, mk, mv, rows_per_mem, tq):
    n_sets = mk.shape[0]
    nrb = rows_per_mem // tq
    return pl.pallas_call(
        _memattn_kernel,
        grid=(n_sets, nrb),
        in_specs=[
            pl.BlockSpec((tq, W_MEM), lambda b, r: (b * nrb + r, 0)),
            pl.BlockSpec((None, N_MEM, W_MEM), lambda b, r: (b, 0, 0)),
            pl.BlockSpec((None, N_MEM, W_MEM), lambda b, r: (b, 0, 0)),
        ],
        out_specs=pl.BlockSpec((tq, W_MEM), lambda b, r: (b * nrb + r, 0)),
        out_shape=jax.ShapeDtypeStruct(q.shape, F32),
        compiler_params=_params(2),
        name="mem_attend",
    )(q, mk, mv)


def _mix_kernel(h_ref, osb_ref, omem_ref, ug_ref, vn_ref, gate_ref, wt_ref, bt_ref,
                wbr_ref, wo_ref, out_ref, *, chunk):
    tm = h_ref.shape[0]
    shift = chunk.bit_length() - 1
    r = lax.broadcasted_iota(jnp.int32, (tm, tm), 0)
    c = lax.broadcasted_iota(jnp.int32, (tm, tm), 1)
    causal = (c <= r) & ((r >> shift) == (c >> shift))
    lane = lax.broadcasted_iota(jnp.int32, (tm, LANE), 1)
    gw = W_SP // SP_GROUPS
    vnb = vn_ref[...].astype(BF16)
    tiles = []
    for j in range(W_SP // LANE):
        vt = vnb[:, j * LANE:(j + 1) * LANE]
        per_group = []
        for g in range(LANE // gw):
            w = jnp.where(causal, wt_ref[j * (LANE // gw) + g], 0.0).astype(BF16)
            per_group.append(_dot(w, vt))
        tiles.append(jnp.where(lane < gw, per_group[0], per_group[1]))
    s = jnp.concatenate(tiles, axis=1) + bt_ref[...]
    osp = ug_ref[...].astype(F32) * s

    mixed = None
    for i, o in enumerate((osb_ref[...], osp, omem_ref[...])):
        br = _dot(o.astype(BF16), wbr_ref[i]) * gate_ref[:, i * D_MODEL:(i + 1) * D_MODEL]
        mixed = br if mixed is None else mixed + br
    out_ref[...] = h_ref[...] + _dot(mixed.astype(BF16), wo_ref[...])


def _mix(h, osb, omem, ug, vn, gates, wt, bt, wbr, wo, chunk):
    rows = h.shape[0]
    tm = ROW_BLOCK
    row = lambda i: (i, 0)
    return pl.pallas_call(
        functools.partial(_mix_kernel, chunk=chunk),
        grid=(rows // tm,),
        in_specs=[
            pl.BlockSpec((tm, D_MODEL), row),
            pl.BlockSpec((tm, W_SB), row),
            pl.BlockSpec((tm, W_MEM), row),
            pl.BlockSpec((tm, W_SP), row),
            pl.BlockSpec((tm, W_SP), row),
            pl.BlockSpec((tm, N_BRANCH * D_MODEL), row),
            _resident((SP_GROUPS, tm, tm), lambda i: (0, 0, 0)),
            _resident((tm, W_SP), lambda i: (0, 0)),
            _resident((N_BRANCH, W_SB, D_MODEL), lambda i: (0, 0, 0)),
            _resident((D_MODEL, D_MODEL), lambda i: (0, 0)),
        ],
        out_specs=pl.BlockSpec((tm, D_MODEL), row),
        out_shape=jax.ShapeDtypeStruct((rows, D_MODEL), F32),
        compiler_params=_params(1),
        name="mix",
    )(h, osb, omem, ug, vn, gates, wt, bt, wbr, wo)


def _ffn_kernel(*refs, seq_len, final_norm):
    h_ref, g_ref, wup_ref, cw_ref, cb_ref, wdn_ref, gf_ref = refs[:7]
    if seq_len is None:
        out_ref, up_ref, act_ref, carry_ref = refs[7:]
    else:
        s1_ref, s2_ref = refs[7:9]
        out_ref, up_ref, act_ref = refs[9:]
    tm = h_ref.shape[0]
    i = pl.program_id(0)
    h = h_ref[...]
    f = _rms(h, g_ref[...]).astype(BF16)
    pos = lax.broadcasted_iota(jnp.int32, (tm, FFN_COLS), 0)
    if seq_len is None:
        @pl.when(i == 0)
        def _():
            carry_ref[...] = jnp.zeros_like(carry_ref)
    else:
        pos = pos % seq_len

    for c in range(D_FF // FFN_COLS):
        conv = []
        for part in range(2):
            cols = slice(part * D_FF + c * FFN_COLS, part * D_FF + (c + 1) * FFN_COLS)
            up = _dot(f, wup_ref[:, cols])
            if seq_len is None:
                prev = carry_ref[:, cols]
                p1 = jnp.broadcast_to(prev[SUBLANE - 1:SUBLANE, :], up.shape)
                p2 = jnp.where(pos == 0, prev[SUBLANE - 2:SUBLANE - 1, :], p1)
                carry_ref[:, cols] = up[tm - SUBLANE:, :]
                up_ref[:, cols] = up[tm - SUBLANE:, :]
            else:
                p1 = s1_ref[:, cols]
                p2 = s2_ref[:, cols]
                up_ref[:, cols] = up
            x1 = jnp.where(pos >= 1, pltpu.roll(up, 1, 0), p1)
            x2 = jnp.where(pos >= 2, pltpu.roll(up, 2, 0), p2)
            conv.append(cb_ref[:, cols] + cw_ref[0:1, cols] * x2 + cw_ref[1:2, cols] * x1
                        + cw_ref[2:3, cols] * up)
        act_ref[:, c * FFN_COLS:(c + 1) * FFN_COLS] = (_gelu(conv[0]) * conv[1]).astype(BF16)

    y = h + _dot(act_ref[...], wdn_ref[...])
    if final_norm:
        y = _rms(y, gf_ref[...])
    out_ref[...] = y


def _ffn(h, gain, wup, cw, cb, wdn, gfinal, prev_rows, seq_len, final_norm):
    rows = h.shape[0]
    tm = ROW_BLOCK
    row = lambda i: (i, 0)
    const = lambda i: (0, 0)
    in_specs = [
        pl.BlockSpec((tm, D_MODEL), row),
        _resident((1, D_MODEL), const),
        _resident((D_MODEL, 2 * D_FF), const),
        _resident((CONV_W, 2 * D_FF), const),
        _resident((1, 2 * D_FF), const),
        _resident((D_FF, D_MODEL), const),
        _resident((1, D_MODEL), const),
    ]
    args = [h, gain, wup, cw, cb, wdn, gfinal]
    scratch = [pltpu.VMEM((tm, D_FF), BF16)]
    if seq_len is None:
        up_rows = SUBLANE
        up_spec = pl.BlockSpec((SUBLANE, 2 * D_FF), const)
        scratch.append(pltpu.VMEM((SUBLANE, 2 * D_FF), F32))
    else:
        up_rows = rows
        up_spec = pl.BlockSpec((tm, 2 * D_FF), row)
        in_specs += [_resident((tm, 2 * D_FF), row)] * 2
        args += list(prev_rows)
    return pl.pallas_call(
        functools.partial(_ffn_kernel, seq_len=seq_len, final_norm=final_norm),
        grid=(rows // tm,),
        in_specs=in_specs,
        out_specs=(pl.BlockSpec((tm, D_MODEL), row), up_spec),
        out_shape=(jax.ShapeDtypeStruct((rows, D_MODEL), F32),
                   jax.ShapeDtypeStruct((up_rows, 2 * D_FF), F32)),
        scratch_shapes=scratch,
        compiler_params=_params(1),
        name="ffn",
    )(*args)


def _augment(w, scale):
    w = (w * scale).reshape(D_MODEL, H_SB, D_SB)
    return jnp.pad(w, ((0, 0), (0, 0), (0, AUG - D_SB))).reshape(D_MODEL, H_SB * AUG)


def _fused_in_weight(w_in_l):
    sizes = [W_SB, W_SB, W_SB, W_SP, W_SP, W_MEM, N_BRANCH * D_MODEL]
    offs = [0]
    for s in sizes:
        offs.append(offs[-1] + s)
    wq, wk, wv, wu, wvs, wqm, wg = [w_in_l[:, offs[i]:offs[i + 1]] for i in range(7)]
    parts = [_augment(wq, LOG2E * D_SB ** -0.5), _augment(wk, 1.0), wk, wv, wu, wvs,
             wqm * D_MEM ** -0.5, wg]
    return jnp.concatenate(parts, axis=1).astype(BF16)


def _bias_row(sb_bias_l):
    b2 = sb_bias_l.astype(F32) * LOG2E
    hi = b2.astype(BF16).astype(F32)
    lo = (b2 - hi).astype(BF16).astype(F32)
    zq = jnp.zeros((H_SB, AUG), F32).at[:, D_SB].set(1.0).at[:, D_SB + 1].set(1.0)
    zk = jnp.zeros((H_SB, AUG), F32).at[:, D_SB].set(hi).at[:, D_SB + 1].set(lo)
    return jnp.concatenate([zq.reshape(1, -1), zk.reshape(1, -1)], axis=1)


def _spatial_operands(sp_w_l, sp_b_l, chunk, tm):
    reps = tm // chunk
    wt = jnp.tile(sp_w_l[:, :chunk, :chunk], (1, reps, reps))
    bt = jnp.tile(jnp.repeat(sp_b_l[:, :chunk].T, W_SP // SP_GROUPS, axis=1), (reps, 1))
    return wt, bt


def kernel(x_prompt, x_sample, mem_prompt, cache_sb_k, cache_sb_v, page_table, cache_mem_k,
           cache_mem_v, state_ffn_conv, norm_mix, w_in, sb_bias, sp_norm, sp_w, sp_b, mem_norm,
           w_mem_kv, w_branch, w_o, norm_ffn, w_up, conv_w, conv_b, w_down, norm_final):
    depth = w_in.shape[0]
    batch, seq, _ = x_prompt.shape
    n_seq, dec_seq, _ = x_sample.shape
    assert batch == 1 and dec_seq == SUBLANE and n_seq * dec_seq == ROW_BLOCK
    assert seq % ROW_BLOCK == 0 and page_table.shape[1] % PAGES_PER_STEP == 0
    n_pool = cache_sb_k.shape[1]

    hp = x_prompt.reshape(seq, D_MODEL)
    hs = x_sample.reshape(n_seq * dec_seq, D_MODEL)
    cache_k = cache_sb_k.reshape(depth, n_pool, PAGE_SIZE, W_SB)
    cache_v = cache_sb_v.reshape(depth, n_pool, PAGE_SIZE, W_SB)
    mem_kv = _memkv(mem_prompt.reshape(N_MEM, D_MODEL), mem_norm, w_mem_kv.astype(BF16))
    gfinal = norm_final.reshape(1, D_MODEL)
    head_eye = jnp.eye(H_SB, dtype=BF16)

    pk, pv, sk, sv, spv, pconv, sconv = [], [], [], [], [], [], []
    for l in range(depth):
        w_all = _fused_in_weight(w_in[l])
        brow = _bias_row(sb_bias[l])
        gain_mix = norm_mix[l].reshape(1, D_MODEL)
        spn = sp_norm[l].reshape(1, W_SP)
        wbr = w_branch[l].astype(BF16)
        wo = w_o[l].astype(BF16)
        gain_ffn = norm_ffn[l].reshape(1, D_MODEL)
        wup = w_up[l].astype(BF16)
        wdn = w_down[l].astype(BF16)
        cw = conv_w[l]
        cb = conv_b[l].reshape(1, 2 * D_FF)
        last = l == depth - 1

        qT, ka, k_p, v_p, vT, ug, vn, qm, gates = _inproj(hp, gain_mix, w_all, brow, spn)
        osb = _sb_prompt(qT, ka, vT)
        mk = mem_kv[l:l + 1, :, :W_MEM]
        mv = mem_kv[l:l + 1, :, W_MEM:]
        omem = _mem_attend(qm, mk, mv, seq, 2 * ROW_BLOCK)
        wt, bt = _spatial_operands(sp_w[l], sp_b[l], CHUNK, ROW_BLOCK)
        hp = _mix(hp, osb, omem, ug, vn, gates, wt, bt, wbr, wo, CHUNK)
        hp, up_last = _ffn(hp, gain_ffn, wup, cw, cb, wdn, gfinal, None, None, last)
        pk.append(k_p)
        pv.append(v_p)
        pconv.append(up_last[SUBLANE - (CONV_W - 1):, :])

        qT, ka, k_s, v_s, vT, ug, vn, qm, gates = _inproj(hs, gain_mix, w_all, brow, spn)
        q = qT.reshape(H_SB, AUG, n_seq, dec_seq)[:, :D_SB]
        qbd = jnp.einsum('hdbt,hg->bhtgd', q, head_eye).reshape(n_seq, H_SB * dec_seq, W_SB)
        bias_full = jnp.broadcast_to(
            jnp.repeat(sb_bias[l].astype(F32) * LOG2E, dec_seq)[:, None], (H_SB * dec_seq, PAGE_SIZE))
        osb = _sb_sample(l, page_table, qbd, bias_full, k_s, v_s, cache_k, cache_v)
        mk = cache_mem_k[l].reshape(n_seq, N_MEM, W_MEM)
        mv = cache_mem_v[l].reshape(n_seq, N_MEM, W_MEM)
        omem = _mem_attend(qm, mk, mv, dec_seq, dec_seq)
        wt, bt = _spatial_operands(sp_w[l], sp_b[l], dec_seq, ROW_BLOCK)
        hs = _mix(hs, osb, omem, ug, vn, gates, wt, bt, wbr, wo, dec_seq)
        st = state_ffn_conv[l]
        s1 = jnp.repeat(st[:, 1:2, :], dec_seq, axis=1).reshape(n_seq * dec_seq, 2 * D_FF)
        s2 = jnp.tile(st, (1, dec_seq // (CONV_W - 1), 1)).reshape(n_seq * dec_seq, 2 * D_FF)
        hs, up_s = _ffn(hs, gain_ffn, wup, cw, cb, wdn, gfinal, (s1, s2), dec_seq, last)
        sk.append(k_s)
        sv.append(v_s)
        spv.append(vn)
        sconv.append(up_s.reshape(n_seq, dec_seq, 2 * D_FF)[:, dec_seq - (CONV_W - 1):, :])

    n_pp = seq // PAGE_SIZE
    y_prompt = hp.reshape(batch, seq, D_MODEL)
    y_sample = hs.reshape(n_seq, dec_seq, D_MODEL)
    return (
        y_prompt,
        y_sample,
        jnp.stack(pk).reshape(depth, n_pp, PAGE_SIZE, H_SB, D_SB),
        jnp.stack(pv).reshape(depth, n_pp, PAGE_SIZE, H_SB, D_SB),
        jnp.stack(sk).reshape(depth, n_seq, dec_seq, H_SB, D_SB),
        jnp.stack(sv).reshape(depth, n_seq, dec_seq, H_SB, D_SB),
        mem_kv[:, :, :W_MEM].reshape(depth, batch, N_MEM, H_MEM, D_MEM),
        mem_kv[:, :, W_MEM:].reshape(depth, batch, N_MEM, H_MEM, D_MEM),
        jnp.stack(spv).reshape(depth, n_seq, dec_seq, W_SP),
        jnp.stack(pconv).reshape(depth, batch, CONV_W - 1, 2 * D_FF),
        jnp.stack(sconv).reshape(depth, n_seq, CONV_W - 1, 2 * D_FF),
    )
```

```python
import functools
import math

import jax
import jax.numpy as jnp
from jax import lax
from jax.experimental import pallas as pl
from jax.experimental.pallas import tpu as pltpu

F32 = jnp.float32
BF16 = jnp.bfloat16

D_MODEL = 1024
H_SB = 8
D_SB = 64
W_SB = H_SB * D_SB
SP_GROUPS = 8
W_SP = 512
CHUNK = 128
N_MEM = 256
H_MEM = 4
D_MEM = 128
W_MEM = H_MEM * D_MEM
N_BRANCH = 3
D_FF = 2816
CONV_W = 3
EPS = 1e-6
PAGE_SIZE = 128

LOG2E = 1.4426950408889634
MAX_LOGIT2 = 126.0
LANE = 128
SUBLANE = 8
ROW_BLOCK = 256
AUG = 2 * D_SB
FFN_COLS = 256
SB_HEADS = 4
SB_KEY_TILES = 2
SB_STAGE_LAG = 2
PAGES_PER_STEP = 8
VMEM_LIMIT = 56 * 1024 * 1024

C_QA = 0
C_KA = C_QA + H_SB * AUG
C_K = C_KA + H_SB * AUG
C_V = C_K + W_SB
C_U = C_V + W_SB
C_VS = C_U + W_SP
C_QM = C_VS + W_SP
C_G = C_QM + W_MEM
C_END = C_G + N_BRANCH * D_MODEL


def _params(n_axes, limit=VMEM_LIMIT):
    return pltpu.CompilerParams(
        dimension_semantics=("arbitrary",) * n_axes, vmem_limit_bytes=limit)


def _resident(shape, index_map):
    return pl.BlockSpec(shape, index_map, pipeline_mode=pl.Buffered(1))


def _rms(x, gain):
    return x * lax.rsqrt(jnp.mean(x * x, axis=-1, keepdims=True) + EPS) * gain


def _gelu(x):
    c = math.sqrt(2.0 / math.pi)
    return x * (0.5 * (1.0 + jnp.tanh(c * (x + 0.044715 * (x * x * x)))))


def _dot(a, b):
    return jnp.dot(a, b, preferred_element_type=F32)


def _softplus2(z):
    return jnp.log2(1.0 + jnp.exp2(z))


def _memkv_kernel(mem_ref, g_ref, w_ref, kv_ref):
    a = _rms(mem_ref[...], g_ref[...]).astype(BF16)
    kv_ref[...] = _dot(a, w_ref[...])


def _memkv(mem, mem_norm, w_mem_kv_bf):
    depth = w_mem_kv_bf.shape[0]
    return pl.pallas_call(
        _memkv_kernel,
        grid=(depth,),
        in_specs=[
            pl.BlockSpec((N_MEM, D_MODEL), lambda l: (0, 0)),
            pl.BlockSpec((None, 1, D_MODEL), lambda l: (l, 0, 0)),
            pl.BlockSpec((None, D_MODEL, 2 * W_MEM), lambda l: (l, 0, 0)),
        ],
        out_specs=pl.BlockSpec((None, N_MEM, 2 * W_MEM), lambda l: (l, 0, 0)),
        out_shape=jax.ShapeDtypeStruct((depth, N_MEM, 2 * W_MEM), F32),
        compiler_params=_params(1),
        name="memkv",
    )(mem, mem_norm.reshape(depth, 1, D_MODEL), w_mem_kv_bf)


def _inproj_kernel(x_ref, g_ref, w_ref, brow_ref, spn_ref,
                   qT_ref, ka_ref, k_ref, v_ref, vT_ref, ug_ref, vn_ref, qm_ref, gate_ref):
    a = _rms(x_ref[...], g_ref[...]).astype(BF16)

    def mm(c0, c1):
        return _dot(a, w_ref[:, c0:c1])

    qa = mm(C_QA, C_KA) + brow_ref[:, :H_SB * AUG]
    qT_ref[...] = qa.T.astype(BF16)
    ka = mm(C_KA, C_K) + brow_ref[:, H_SB * AUG:]
    ka_ref[...] = ka.astype(BF16)
    k_ref[...] = mm(C_K, C_V)
    v = mm(C_V, C_U)
    v_ref[...] = v
    vT_ref[...] = v.T.astype(BF16)
    ug_ref[...] = _gelu(mm(C_U, C_VS)).astype(BF16)
    vn_ref[...] = _rms(_gelu(mm(C_VS, C_QM)), spn_ref[...])
    qm_ref[...] = mm(C_QM, C_G)
    for i in range(N_BRANCH):
        g = mm(C_G + i * D_MODEL, C_G + (i + 1) * D_MODEL)
        gate_ref[:, i * D_MODEL:(i + 1) * D_MODEL] = jax.nn.sigmoid(g)


def _inproj(x, gain, w_all, brow, sp_norm):
    rows = x.shape[0]
    tm = ROW_BLOCK
    nb = rows // tm
    row = lambda i: (i, 0)
    const = lambda i: (0, 0)
    out_shapes = (
        jax.ShapeDtypeStruct((nb, H_SB * AUG, tm), BF16),
        jax.ShapeDtypeStruct((rows, H_SB * AUG), BF16),
        jax.ShapeDtypeStruct((rows, W_SB), F32),
        jax.ShapeDtypeStruct((rows, W_SB), F32),
        jax.ShapeDtypeStruct((nb, W_SB, tm), BF16),
        jax.ShapeDtypeStruct((rows, W_SP), BF16),
        jax.ShapeDtypeStruct((rows, W_SP), F32),
        jax.ShapeDtypeStruct((rows, W_MEM), F32),
        jax.ShapeDtypeStruct((rows, N_BRANCH * D_MODEL), F32),
    )
    out_specs = (
        pl.BlockSpec((None, H_SB * AUG, tm), lambda i: (i, 0, 0)),
        pl.BlockSpec((tm, H_SB * AUG), row),
        pl.BlockSpec((tm, W_SB), row),
        pl.BlockSpec((tm, W_SB), row),
        pl.BlockSpec((None, W_SB, tm), lambda i: (i, 0, 0)),
        pl.BlockSpec((tm, W_SP), row),
        pl.BlockSpec((tm, W_SP), row),
        pl.BlockSpec((tm, W_MEM), row),
        pl.BlockSpec((tm, N_BRANCH * D_MODEL), row),
    )
    return pl.pallas_call(
        _inproj_kernel,
        grid=(nb,),
        in_specs=[
            pl.BlockSpec((tm, D_MODEL), row),
            _resident((1, D_MODEL), const),
            _resident((D_MODEL, C_END), const),
            _resident((1, 2 * H_SB * AUG), const),
            _resident((1, W_SP), const),
        ],
        out_specs=out_specs,
        out_shape=out_shapes,
        compiler_params=_params(1),
        name="inproj",
    )(x, gain, w_all, brow, sp_norm)


def _sb_chains(chains, upper):
    depth = max(len(tiles) for _, tiles in chains)
    order = [(c, p) for p in range(depth) for c, (_, tiles) in enumerate(chains) if p < len(tiles)]
    carries = [carry for carry, _ in chains]
    totals = [None] * len(chains)
    zs, mids = {}, {}
    for step in range(len(order) + 2 * SB_STAGE_LAG):
        if step < len(order):
            c, p = order[step]
            k, q, _, _ = chains[c][1][p]
            zs[c, p] = jnp.minimum(_dot(k, q), MAX_LOGIT2)
        if 0 <= step - SB_STAGE_LAG < len(order):
            c, p = order[step - SB_STAGE_LAG]
            valid = chains[c][1][p][3]
            z = zs.pop((c, p))
            sp = _softplus2(z)
            log_beta = z - sp
            if valid is not None:
                sp = jnp.where(valid, sp, 0.0)
            newer = _dot(upper, sp.astype(BF16))
            mids[c, p] = (log_beta, newer, sp[0:1, :])
        if 0 <= step - 2 * SB_STAGE_LAG < len(order):
            c, p = order[step - 2 * SB_STAGE_LAG]
            _, _, vT, valid = chains[c][1][p]
            log_beta, newer, sp0 = mids.pop((c, p))
            a = jnp.exp2(log_beta - newer - carries[c])
            if valid is not None:
                a = jnp.where(valid, a, 0.0)
            o = _dot(vT, a.astype(BF16))
            totals[c] = o if totals[c] is None else totals[c] + o
            carries[c] = carries[c] + newer[0:1, :] + sp0
    return list(zip(totals, carries))


def _sbp_kernel(qT_ref, ka_ref, vT_ref, o_ref, acc_ref):
    t = ROW_BLOCK
    qi = pl.program_id(1)
    rk = lax.broadcasted_iota(jnp.int32, (t, t), 0)
    cq = lax.broadcasted_iota(jnp.int32, (t, t), 1)
    upper = jnp.where(cq > rk, 1.0, 0.0).astype(BF16)
    causal = rk < cq
    zero = jnp.zeros((1, t), F32)
    streams = [(hh, half) for hh in range(SB_HEADS) for half in range(2)]

    def tile(kb, hh, half, valid):
        r0 = pl.multiple_of(kb * t, t)
        return (ka_ref[pl.ds(r0, t), hh * AUG:(hh + 1) * AUG],
                qT_ref[half, hh * AUG:(hh + 1) * AUG, :],
                vT_ref[kb, hh * D_SB:(hh + 1) * D_SB, :], valid)

    def acc_at(hh, half):
        return (slice(hh * D_SB, (hh + 1) * D_SB), slice(half * t, (half + 1) * t))

    hi, lo = 2 * qi + 1, 2 * qi
    chains = []
    for hh, half in streams:
        if half == 0:
            chains.append((zero, [tile(lo, hh, 0, causal)]))
        else:
            chains.append((zero, [tile(hi, hh, 1, causal), tile(lo, hh, 1, None)]))
    carries = []
    for (hh, half), (o, carry) in zip(streams, _sb_chains(chains, upper)):
        acc_ref[acc_at(hh, half)] = o
        carries.append(carry)

    def body(jj, carries):
        kb0 = lo - 1 - jj * SB_KEY_TILES
        chains = [(carries[i], [tile(kb0 - d, hh, half, None) for d in range(SB_KEY_TILES)])
                  for i, (hh, half) in enumerate(streams)]
        out = []
        for (hh, half), (o, carry) in zip(streams, _sb_chains(chains, upper)):
            acc_ref[acc_at(hh, half)] += o
            out.append(carry)
        return tuple(out)

    lax.fori_loop(0, lo // SB_KEY_TILES, body, tuple(carries))
    o_ref[...] = acc_ref[...].T


def _sb_prompt(qT, ka, vT):
    nb = qT.shape[0]
    t = ROW_BLOCK
    rows = nb * t
    return pl.pallas_call(
        _sbp_kernel,
        grid=(H_SB // SB_HEADS, nb // 2),
        in_specs=[
            pl.BlockSpec((2, SB_HEADS * AUG, t), lambda hg, qi: (qi, hg, 0)),
            _resident((rows, SB_HEADS * AUG), lambda hg, qi: (0, hg)),
            _resident((nb, SB_HEADS * D_SB, t), lambda hg, qi: (0, hg, 0)),
        ],
        out_specs=pl.BlockSpec((2 * t, SB_HEADS * D_SB), lambda hg, qi: (qi, hg)),
        out_shape=jax.ShapeDtypeStruct((rows, W_SB), F32),
        scratch_shapes=[pltpu.VMEM((SB_HEADS * D_SB, 2 * t), F32)],
        compiler_params=_params(2),
        name="sb_prompt",
    )(qT, ka, vT)


def _sbs_kernel(pt_ref, qbd_ref, bias_ref, kn_ref, vn_ref, *refs, n_steps):
    pps = PAGES_PER_STEP
    k_refs = refs[:pps]
    v_refs = refs[pps:2 * pps]
    o_ref = refs[2 * pps]
    acc_ref, carry_ref = refs[2 * pps + 1:]
    del pt_ref
    p = pl.program_id(1)
    nq = H_SB * SUBLANE
    qbd = qbd_ref[...]
    bias = bias_ref[...]
    rj = lax.broadcasted_iota(jnp.int32, (PAGE_SIZE, 2 * PAGE_SIZE), 0)
    cs = lax.broadcasted_iota(jnp.int32, (PAGE_SIZE, 2 * PAGE_SIZE), 1)
    newer_tot = jnp.where((rj > cs) | (cs >= PAGE_SIZE), 1.0, 0.0).astype(BF16)

    nt_dims = (((1,), (1,)), ((), ()))

    def tiles(blocks, valid, keys_on_lanes):
        zs = []
        for k_blk, _ in blocks:
            kb = k_blk.astype(BF16)
            if keys_on_lanes:
                zs.append(_dot(qbd, kb))
            else:
                zs.append(lax.dot_general(qbd, kb, nt_dims, preferred_element_type=F32))
        mids = []
        for z in zs:
            z = jnp.minimum(z + bias, MAX_LOGIT2)
            sp = _softplus2(z)
            log_beta = z - sp
            if valid is not None:
                sp = jnp.where(valid, sp, 0.0)
            mids.append((log_beta, _dot(sp.astype(BF16), newer_tot)))
        carry = carry_ref[...]
        acc = acc_ref[...]
        for (_, v_blk), (log_beta, nt) in zip(blocks, mids):
            a = jnp.exp2(log_beta - nt[:, :PAGE_SIZE] - carry)
            if valid is not None:
                a = jnp.where(valid, a, 0.0)
            ab = a.astype(BF16)
            vb = v_blk.astype(BF16)
            if keys_on_lanes:
                acc = acc + lax.dot_general(ab, vb, nt_dims, preferred_element_type=F32)
            else:
                acc = acc + _dot(ab, vb)
            carry = carry + nt[:, PAGE_SIZE:]
        carry_ref[...] = carry
        acc_ref[...] = acc

    @pl.when(p == 0)
    def _():
        acc_ref[...] = jnp.zeros_like(acc_ref)
        carry_ref[...] = jnp.zeros_like(carry_ref)
        pad = jnp.zeros((PAGE_SIZE - SUBLANE, W_SB), F32)
        kn = jnp.concatenate([kn_ref[...], pad], axis=0)
        vn = jnp.concatenate([vn_ref[...], pad], axis=0)
        tq = lax.broadcasted_iota(jnp.int32, (nq, PAGE_SIZE), 0) % SUBLANE
        kj = lax.broadcasted_iota(jnp.int32, (nq, PAGE_SIZE), 1)
        tiles([(kn, vn)], kj < tq, False)

    tiles([(k_refs[i][...], v_refs[i][...]) for i in range(pps)], None, True)

    @pl.when(p == n_steps - 1)
    def _():
        acc = acc_ref[...]
        rh = lax.broadcasted_iota(jnp.int32, acc.shape, 0) // SUBLANE
        ch = lax.broadcasted_iota(jnp.int32, acc.shape, 1) // D_SB
        own = jnp.where(rh == ch, acc, 0.0)
        out = own[0:SUBLANE, :]
        for h in range(1, H_SB):
            out = out + own[h * SUBLANE:(h + 1) * SUBLANE, :]
        o_ref[...] = out


def _sb_sample(layer, page_table, qbd, bias_full, k_new, v_new, cache_k, cache_v):
    n_seq, n_pages = page_table.shape
    pps = PAGES_PER_STEP
    n_steps = n_pages // pps
    nq = H_SB * SUBLANE

    def page_spec(i):
        def imap(b, p, pt):
            return (layer, pt[b, n_pages - 1 - (p * pps + i)], 0, 0)
        return pl.BlockSpec((None, None, W_SB, PAGE_SIZE), imap)

    grid_spec = pltpu.PrefetchScalarGridSpec(
        num_scalar_prefetch=1,
        grid=(n_seq, n_steps),
        in_specs=[
            pl.BlockSpec((None, nq, W_SB), lambda b, p, pt: (b, 0, 0)),
            pl.BlockSpec((nq, PAGE_SIZE), lambda b, p, pt: (0, 0)),
            pl.BlockSpec((SUBLANE, W_SB), lambda b, p, pt: (b, 0)),
            pl.BlockSpec((SUBLANE, W_SB), lambda b, p, pt: (b, 0)),
        ] + [page_spec(i) for i in range(pps)] * 2,
        out_specs=pl.BlockSpec((SUBLANE, W_SB), lambda b, p, pt: (b, 0)),
        scratch_shapes=[pltpu.VMEM((nq, W_SB), F32), pltpu.VMEM((nq, PAGE_SIZE), F32)],
    )
    return pl.pallas_call(
        functools.partial(_sbs_kernel, n_steps=n_steps),
        grid_spec=grid_spec,
        out_shape=jax.ShapeDtypeStruct((n_seq * SUBLANE, W_SB), F32),
        compiler_params=_params(2),
        name="sb_sample",
    )(page_table, qbd, bias_full, k_new, v_new,
      *([cache_k] * pps), *([cache_v] * pps))


def _memattn_kernel(q_ref, mk_ref, mv_ref, o_ref):
    for h in range(H_MEM):
        cols = slice(h * D_MEM, (h + 1) * D_MEM)
        q = q_ref[:, cols].astype(BF16)
        s = lax.dot_general(q, mk_ref[:, cols].astype(BF16), (((1,), (1,)), ((), ())),
                            preferred_element_type=F32)
        e = jnp.exp(s - jnp.max(s, axis=-1, keepdims=True))
        p = e / jnp.sum(e, axis=-1, keepdims=True)
        o_ref[:, cols] = _dot(p.astype(BF16), mv_ref[:, cols].astype(BF16))


def _mem_attend(q, mk, mv, rows_per_mem, tq):
    n_sets = mk.shape[0]
    nrb = rows_per_mem // tq
    return pl.pallas_call(
        _memattn_kernel,
        grid=(n_sets, nrb),
        in_specs=[
            pl.BlockSpec((tq, W_MEM), lambda b, r: (b * nrb + r, 0)),
            pl.BlockSpec((None, N_MEM, W_MEM), lambda b, r: (b, 0, 0)),
            pl.BlockSpec((None, N_MEM, W_MEM), lambda b, r: (b, 0, 0)),
        ],
        out_specs=pl.BlockSpec((tq, W_MEM), lambda b, r: (b * nrb + r, 0)),
        out_shape=jax.ShapeDtypeStruct(q.shape, F32),
        compiler_params=_params(2),
        name="mem_attend",
    )(q, mk, mv)


def _mix_kernel(h_ref, osb_ref, omem_ref, ug_ref, vn_ref, gate_ref, wt_ref, bt_ref,
                wbr_ref, wo_ref, out_ref, *, chunk):
    tm = h_ref.shape[0]
    shift = chunk.bit_length() - 1
    r = lax.broadcasted_iota(jnp.int32, (tm, tm), 0)
    c = lax.broadcasted_iota(jnp.int32, (tm, tm), 1)
    causal = (c <= r) & ((r >> shift) == (c >> shift))
    lane = lax.broadcasted_iota(jnp.int32, (tm, LANE), 1)
    gw = W_SP // SP_GROUPS
    vnb = vn_ref[...].astype(BF16)
    tiles = []
    for j in range(W_SP // LANE):
        vt = vnb[:, j * LANE:(j + 1) * LANE]
        per_group = []
        for g in range(LANE // gw):
            w = jnp.where(causal, wt_ref[j * (LANE // gw) + g], 0.0).astype(BF16)
            per_group.append(_dot(w, vt))
        tiles.append(jnp.where(lane < gw, per_group[0], per_group[1]))
    s = jnp.concatenate(tiles, axis=1) + bt_ref[...]
    osp = ug_ref[...].astype(F32) * s

    mixed = None
    for i, o in enumerate((osb_ref[...], osp, omem_ref[...])):
        br = _dot(o.astype(BF16), wbr_ref[i]) * gate_ref[:, i * D_MODEL:(i + 1) * D_MODEL]
        mixed = br if mixed is None else mixed + br
    out_ref[...] = h_ref[...] + _dot(mixed.astype(BF16), wo_ref[...])


def _mix(h, osb, omem, ug, vn, gates, wt, bt, wbr, wo, chunk):
    rows = h.shape[0]
    tm = ROW_BLOCK
    row = lambda i: (i, 0)
    return pl.pallas_call(
        functools.partial(_mix_kernel, chunk=chunk),
        grid=(rows // tm,),
        in_specs=[
            pl.BlockSpec((tm, D_MODEL), row),
            pl.BlockSpec((tm, W_SB), row),
            pl.BlockSpec((tm, W_MEM), row),
            pl.BlockSpec((tm, W_SP), row),
            pl.BlockSpec((tm, W_SP), row),
            pl.BlockSpec((tm, N_BRANCH * D_MODEL), row),
            _resident((SP_GROUPS, tm, tm), lambda i: (0, 0, 0)),
            _resident((tm, W_SP), lambda i: (0, 0)),
            _resident((N_BRANCH, W_SB, D_MODEL), lambda i: (0, 0, 0)),
            _resident((D_MODEL, D_MODEL), lambda i: (0, 0)),
        ],
        out_specs=pl.BlockSpec((tm, D_MODEL), row),
        out_shape=jax.ShapeDtypeStruct((rows, D_MODEL), F32),
        compiler_params=_params(1),
        name="mix",
    )(h, osb, omem, ug, vn, gates, wt, bt, wbr, wo)


def _ffn_kernel(*refs, seq_len, final_norm):
    h_ref, g_ref, wup_ref, cw_ref, cb_ref, wdn_ref, gf_ref = refs[:7]
    if seq_len is None:
        out_ref, up_ref, act_ref, carry_ref = refs[7:]
    else:
        s1_ref, s2_ref = refs[7:9]
        out_ref, up_ref, act_ref = refs[9:]
    tm = h_ref.shape[0]
    i = pl.program_id(0)
    h = h_ref[...]
    f = _rms(h, g_ref[...]).astype(BF16)
    pos = lax.broadcasted_iota(jnp.int32, (tm, FFN_COLS), 0)
    if seq_len is None:
        @pl.when(i == 0)
        def _():
            carry_ref[...] = jnp.zeros_like(carry_ref)
    else:
        pos = pos % seq_len

    for c in range(D_FF // FFN_COLS):
        conv = []
        for part in range(2):
            cols = slice(part * D_FF + c * FFN_COLS, part * D_FF + (c + 1) * FFN_COLS)
            up = _dot(f, wup_ref[:, cols])
            if seq_len is None:
                prev = carry_ref[:, cols]
                p1 = jnp.broadcast_to(prev[SUBLANE - 1:SUBLANE, :], up.shape)
                p2 = jnp.where(pos == 0, prev[SUBLANE - 2:SUBLANE - 1, :], p1)
                carry_ref[:, cols] = up[tm - SUBLANE:, :]
                up_ref[:, cols] = up[tm - SUBLANE:, :]
            else:
                p1 = s1_ref[:, cols]
                p2 = s2_ref[:, cols]
                up_ref[:, cols] = up
            x1 = jnp.where(pos >= 1, pltpu.roll(up, 1, 0), p1)
            x2 = jnp.where(pos >= 2, pltpu.roll(up, 2, 0), p2)
            conv.append(cb_ref[:, cols] + cw_ref[0:1, cols] * x2 + cw_ref[1:2, cols] * x1
                        + cw_ref[2:3, cols] * up)
        act_ref[:, c * FFN_COLS:(c + 1) * FFN_COLS] = (_gelu(conv[0]) * conv[1]).astype(BF16)

    y = h + _dot(act_ref[...], wdn_ref[...])
    if final_norm:
        y = _rms(y, gf_ref[...])
    out_ref[...] = y


def _ffn(h, gain, wup, cw, cb, wdn, gfinal, prev_rows, seq_len, final_norm):
    rows = h.shape[0]
    tm = ROW_BLOCK
    row = lambda i: (i, 0)
    const = lambda i: (0, 0)
    in_specs = [
        pl.BlockSpec((tm, D_MODEL), row),
        _resident((1, D_MODEL), const),
        _resident((D_MODEL, 2 * D_FF), const),
        _resident((CONV_W, 2 * D_FF), const),
        _resident((1, 2 * D_FF), const),
        _resident((D_FF, D_MODEL), const),
        _resident((1, D_MODEL), const),
    ]
    args = [h, gain, wup, cw, cb, wdn, gfinal]
    scratch = [pltpu.VMEM((tm, D_FF), BF16)]
    if seq_len is None:
        up_rows = SUBLANE
        up_spec = pl.BlockSpec((SUBLANE, 2 * D_FF), const)
        scratch.append(pltpu.VMEM((SUBLANE, 2 * D_FF), F32))
    else:
        up_rows = rows
        up_spec = pl.BlockSpec((tm, 2 * D_FF), row)
        in_specs += [_resident((tm, 2 * D_FF), row)] * 2
        args += list(prev_rows)
    return pl.pallas_call(
        functools.partial(_ffn_kernel, seq_len=seq_len, final_norm=final_norm),
        grid=(rows // tm,),
        in_specs=in_specs,
        out_specs=(pl.BlockSpec((tm, D_MODEL), row), up_spec),
        out_shape=(jax.ShapeDtypeStruct((rows, D_MODEL), F32),
                   jax.ShapeDtypeStruct((up_rows, 2 * D_FF), F32)),
        scratch_shapes=scratch,
        compiler_params=_params(1),
        name="ffn",
    )(*args)


def _augment(w, scale):
    w = (w * scale).reshape(D_MODEL, H_SB, D_SB)
    return jnp.pad(w, ((0, 0), (0, 0), (0, AUG - D_SB))).reshape(D_MODEL, H_SB * AUG)


def _fused_in_weight(w_in_l):
    sizes = [W_SB, W_SB, W_SB, W_SP, W_SP, W_MEM, N_BRANCH * D_MODEL]
    offs = [0]
    for s in sizes:
        offs.append(offs[-1] + s)
    wq, wk, wv, wu, wvs, wqm, wg = [w_in_l[:, offs[i]:offs[i + 1]] for i in range(7)]
    parts = [_augment(wq, LOG2E * D_SB ** -0.5), _augment(wk, 1.0), wk, wv, wu, wvs,
             wqm * D_MEM ** -0.5, wg]
    return jnp.concatenate(parts, axis=1).astype(BF16)


def _bias_row(sb_bias_l):
    b2 = sb_bias_l.astype(F32) * LOG2E
    hi = b2.astype(BF16).astype(F32)
    lo = (b2 - hi).astype(BF16).astype(F32)
    zq = jnp.zeros((H_SB, AUG), F32).at[:, D_SB].set(1.0).at[:, D_SB + 1].set(1.0)
    zk = jnp.zeros((H_SB, AUG), F32).at[:, D_SB].set(hi).at[:, D_SB + 1].set(lo)
    return jnp.concatenate([zq.reshape(1, -1), zk.reshape(1, -1)], axis=1)


def _spatial_operands(sp_w_l, sp_b_l, chunk, tm):
    reps = tm // chunk
    wt = jnp.tile(sp_w_l[:, :chunk, :chunk], (1, reps, reps))
    bt = jnp.tile(jnp.repeat(sp_b_l[:, :chunk].T, W_SP // SP_GROUPS, axis=1), (reps, 1))
    return wt, bt


def kernel(x_prompt, x_sample, mem_prompt, cache_sb_k, cache_sb_v, page_table, cache_mem_k,
           cache_mem_v, state_ffn_conv, norm_mix, w_in, sb_bias, sp_norm, sp_w, sp_b, mem_norm,
           w_mem_kv, w_branch, w_o, norm_ffn, w_up, conv_w, conv_b, w_down, norm_final):
    depth = w_in.shape[0]
    batch, seq, _ = x_prompt.shape
    n_seq, dec_seq, _ = x_sample.shape
    assert batch == 1 and dec_seq == SUBLANE and n_seq * dec_seq == ROW_BLOCK
    assert seq % ROW_BLOCK == 0 and page_table.shape[1] % PAGES_PER_STEP == 0

    hp = x_prompt.reshape(seq, D_MODEL)
    hs = x_sample.reshape(n_seq * dec_seq, D_MODEL)
    n_pool = cache_sb_k.shape[1]
    cache_k = jnp.transpose(cache_sb_k, (0, 1, 3, 4, 2)).reshape(depth, n_pool, W_SB, PAGE_SIZE)
    cache_v = jnp.transpose(cache_sb_v, (0, 1, 3, 4, 2)).reshape(depth, n_pool, W_SB, PAGE_SIZE)
    mem_kv = _memkv(mem_prompt.reshape(N_MEM, D_MODEL), mem_norm, w_mem_kv.astype(BF16))
    gfinal = norm_final.reshape(1, D_MODEL)
    head_eye = jnp.eye(H_SB, dtype=BF16)

    pk, pv, sk, sv, spv, pconv, sconv = [], [], [], [], [], [], []
    for l in range(depth):
        w_all = _fused_in_weight(w_in[l])
        brow = _bias_row(sb_bias[l])
        gain_mix = norm_mix[l].reshape(1, D_MODEL)
        spn = sp_norm[l].reshape(1, W_SP)
        wbr = w_branch[l].astype(BF16)
        wo = w_o[l].astype(BF16)
        gain_ffn = norm_ffn[l].reshape(1, D_MODEL)
        wup = w_up[l].astype(BF16)
        wdn = w_down[l].astype(BF16)
        cw = conv_w[l]
        cb = conv_b[l].reshape(1, 2 * D_FF)
        last = l == depth - 1

        qT, ka, k_p, v_p, vT, ug, vn, qm, gates = _inproj(hp, gain_mix, w_all, brow, spn)
        osb = _sb_prompt(qT, ka, vT)
        mk = mem_kv[l:l + 1, :, :W_MEM]
        mv = mem_kv[l:l + 1, :, W_MEM:]
        omem = _mem_attend(qm, mk, mv, seq, 2 * ROW_BLOCK)
        wt, bt = _spatial_operands(sp_w[l], sp_b[l], CHUNK, ROW_BLOCK)
        hp = _mix(hp, osb, omem, ug, vn, gates, wt, bt, wbr, wo, CHUNK)
        hp, up_last = _ffn(hp, gain_ffn, wup, cw, cb, wdn, gfinal, None, None, last)
        pk.append(k_p)
        pv.append(v_p)
        pconv.append(up_last[SUBLANE - (CONV_W - 1):, :])

        qT, ka, k_s, v_s, vT, ug, vn, qm, gates = _inproj(hs, gain_mix, w_all, brow, spn)
        q = qT.reshape(H_SB, AUG, n_seq, dec_seq)[:, :D_SB]
        qbd = jnp.einsum('hdbt,hg->bhtgd', q, head_eye).reshape(n_seq, H_SB * dec_seq, W_SB)
        bias_full = jnp.broadcast_to(
            jnp.repeat(sb_bias[l].astype(F32) * LOG2E, dec_seq)[:, None], (H_SB * dec_seq, PAGE_SIZE))
        osb = _sb_sample(l, page_table, qbd, bias_full, k_s, v_s, cache_k, cache_v)
        mk = cache_mem_k[l].reshape(n_seq, N_MEM, W_MEM)
        mv = cache_mem_v[l].reshape(n_seq, N_MEM, W_MEM)
        omem = _mem_attend(qm, mk, mv, dec_seq, dec_seq)
        wt, bt = _spatial_operands(sp_w[l], sp_b[l], dec_seq, ROW_BLOCK)
        hs = _mix(hs, osb, omem, ug, vn, gates, wt, bt, wbr, wo, dec_seq)
        st = state_ffn_conv[l]
        s1 = jnp.repeat(st[:, 1:2, :], dec_seq, axis=1).reshape(n_seq * dec_seq, 2 * D_FF)
        s2 = jnp.tile(st, (1, dec_seq // (CONV_W - 1), 1)).reshape(n_seq * dec_seq, 2 * D_FF)
        hs, up_s = _ffn(hs, gain_ffn, wup, cw, cb, wdn, gfinal, (s1, s2), dec_seq, last)
        sk.append(k_s)
        sv.append(v_s)
        spv.append(vn)
        sconv.append(up_s.reshape(n_seq, dec_seq, 2 * D_FF)[:, dec_seq - (CONV_W - 1):, :])

    n_pp = seq // PAGE_SIZE
    y_prompt = hp.reshape(batch, seq, D_MODEL)
    y_sample = hs.reshape(n_seq, dec_seq, D_MODEL)
    return (
        y_prompt,
        y_sample,
        jnp.stack(pk).reshape(depth, n_pp, PAGE_SIZE, H_SB, D_SB),
        jnp.stack(pv).reshape(depth, n_pp, PAGE_SIZE, H_SB, D_SB),
        jnp.stack(sk).reshape(depth, n_seq, dec_seq, H_SB, D_SB),
        jnp.stack(sv).reshape(depth, n_seq, dec_seq, H_SB, D_SB),
        mem_kv[:, :, :W_MEM].reshape(depth, batch, N_MEM, H_MEM, D_MEM),
        mem_kv[:, :, W_MEM:].reshape(depth, batch, N_MEM, H_MEM, D_MEM),
        jnp.stack(spv).reshape(depth, n_seq, dec_seq, W_SP),
        jnp.stack(pconv).reshape(depth, batch, CONV_W - 1, 2 * D_FF),
        jnp.stack(sconv).reshape(depth, n_seq, CONV_W - 1, 2 * D_FF),
    )
```

```python
import functools
import math

import jax
import jax.numpy as jnp
from jax import lax
from jax.experimental import pallas as pl
from jax.experimental.pallas import tpu as pltpu

F32 = jnp.float32
BF16 = jnp.bfloat16

D_MODEL = 1024
H_SB = 8
D_SB = 64
W_SB = H_SB * D_SB
SP_GROUPS = 8
W_SP = 512
CHUNK = 128
N_MEM = 256
H_MEM = 4
D_MEM = 128
W_MEM = H_MEM * D_MEM
N_BRANCH = 3
D_FF = 2816
CONV_W = 3
EPS = 1e-6
PAGE_SIZE = 128

LOG2E = 1.4426950408889634
MAX_LOGIT2 = 126.0
LANE = 128
SUBLANE = 8
ROW_BLOCK = 256
AUG = 2 * D_SB
FFN_COLS = 256
SB_HEADS = 4
SB_KEY_TILES = 2
SB_STAGE_LAG = 2
PAGES_PER_STEP = 8
VMEM_LIMIT = 56 * 1024 * 1024

C_QA = 0
C_KA = C_QA + H_SB * AUG
C_K = C_KA + H_SB * AUG
C_V = C_K + W_SB
C_U = C_V + W_SB
C_VS = C_U + W_SP
C_QM = C_VS + W_SP
C_G = C_QM + W_MEM
C_END = C_G + N_BRANCH * D_MODEL


def _params(n_axes, limit=VMEM_LIMIT):
    return pltpu.CompilerParams(
        dimension_semantics=("arbitrary",) * n_axes, vmem_limit_bytes=limit)


def _resident(shape, index_map):
    return pl.BlockSpec(shape, index_map, pipeline_mode=pl.Buffered(1))


def _rms(x, gain):
    return x * lax.rsqrt(jnp.mean(x * x, axis=-1, keepdims=True) + EPS) * gain


def _gelu(x):
    c = math.sqrt(2.0 / math.pi)
    return x * (0.5 * (1.0 + jnp.tanh(c * (x + 0.044715 * (x * x * x)))))


def _dot(a, b):
    return jnp.dot(a, b, preferred_element_type=F32)


def _softplus2(z):
    return jnp.log2(1.0 + jnp.exp2(z))


def _memkv_kernel(mem_ref, g_ref, w_ref, kv_ref):
    a = _rms(mem_ref[...], g_ref[...]).astype(BF16)
    kv_ref[...] = _dot(a, w_ref[...])


def _memkv(mem, mem_norm, w_mem_kv_bf):
    depth = w_mem_kv_bf.shape[0]
    return pl.pallas_call(
        _memkv_kernel,
        grid=(depth,),
        in_specs=[
            pl.BlockSpec((N_MEM, D_MODEL), lambda l: (0, 0)),
            pl.BlockSpec((None, 1, D_MODEL), lambda l: (l, 0, 0)),
            pl.BlockSpec((None, D_MODEL, 2 * W_MEM), lambda l: (l, 0, 0)),
        ],
        out_specs=pl.BlockSpec((None, N_MEM, 2 * W_MEM), lambda l: (l, 0, 0)),
        out_shape=jax.ShapeDtypeStruct((depth, N_MEM, 2 * W_MEM), F32),
        compiler_params=_params(1),
        name="memkv",
    )(mem, mem_norm.reshape(depth, 1, D_MODEL), w_mem_kv_bf)


def _inproj_kernel(x_ref, g_ref, w_ref, brow_ref, spn_ref,
                   qT_ref, ka_ref, k_ref, v_ref, vT_ref, ug_ref, vn_ref, qm_ref, gate_ref, *, paged):
    a = _rms(x_ref[...], g_ref[...]).astype(BF16)

    def mm(c0, c1):
        return _dot(a, w_ref[:, c0:c1])

    qa = mm(C_QA, C_KA) + brow_ref[:, :H_SB * AUG]
    qT_ref[...] = qa.T.astype(BF16)
    ka = mm(C_KA, C_K) + brow_ref[:, H_SB * AUG:]
    ka_ref[...] = ka.astype(BF16)
    k = mm(C_K, C_V)
    v = mm(C_V, C_U)
    vT = v.T
    vT_ref[...] = vT.astype(BF16)
    if paged:
        kT = k.T
        for j in range(k_ref.shape[0]):
            k_ref[j] = kT[:, j * PAGE_SIZE:(j + 1) * PAGE_SIZE]
            v_ref[j] = vT[:, j * PAGE_SIZE:(j + 1) * PAGE_SIZE]
    else:
        k_ref[...] = k
        v_ref[...] = v
    ug_ref[...] = _gelu(mm(C_U, C_VS)).astype(BF16)
    vn_ref[...] = _rms(_gelu(mm(C_VS, C_QM)), spn_ref[...])
    qm_ref[...] = mm(C_QM, C_G)
    for i in range(N_BRANCH):
        g = mm(C_G + i * D_MODEL, C_G + (i + 1) * D_MODEL)
        gate_ref[:, i * D_MODEL:(i + 1) * D_MODEL] = jax.nn.sigmoid(g)


def _inproj(x, gain, w_all, brow, sp_norm, paged):
    rows = x.shape[0]
    tm = ROW_BLOCK
    nb = rows // tm
    row = lambda i: (i, 0)
    const = lambda i: (0, 0)
    if paged:
        ppb = tm // PAGE_SIZE
        kv_shape = jax.ShapeDtypeStruct((rows // PAGE_SIZE, W_SB, PAGE_SIZE), F32)
        kv_spec = pl.BlockSpec((ppb, W_SB, PAGE_SIZE), lambda i: (i, 0, 0))
    else:
        kv_shape = jax.ShapeDtypeStruct((rows, W_SB), F32)
        kv_spec = pl.BlockSpec((tm, W_SB), row)
    out_shapes = (
        jax.ShapeDtypeStruct((nb, H_SB * AUG, tm), BF16),
        jax.ShapeDtypeStruct((rows, H_SB * AUG), BF16),
        kv_shape,
        kv_shape,
        jax.ShapeDtypeStruct((nb, W_SB, tm), BF16),
        jax.ShapeDtypeStruct((rows, W_SP), BF16),
        jax.ShapeDtypeStruct((rows, W_SP), F32),
        jax.ShapeDtypeStruct((rows, W_MEM), F32),
        jax.ShapeDtypeStruct((rows, N_BRANCH * D_MODEL), F32),
    )
    out_specs = (
        pl.BlockSpec((None, H_SB * AUG, tm), lambda i: (i, 0, 0)),
        pl.BlockSpec((tm, H_SB * AUG), row),
        kv_spec,
        kv_spec,
        pl.BlockSpec((None, W_SB, tm), lambda i: (i, 0, 0)),
        pl.BlockSpec((tm, W_SP), row),
        pl.BlockSpec((tm, W_SP), row),
        pl.BlockSpec((tm, W_MEM), row),
        pl.BlockSpec((tm, N_BRANCH * D_MODEL), row),
    )
    return pl.pallas_call(
        functools.partial(_inproj_kernel, paged=paged),
        grid=(nb,),
        in_specs=[
            pl.BlockSpec((tm, D_MODEL), row),
            _resident((1, D_MODEL), const),
            _resident((D_MODEL, C_END), const),
            _resident((1, 2 * H_SB * AUG), const),
            _resident((1, W_SP), const),
        ],
        out_specs=out_specs,
        out_shape=out_shapes,
        compiler_params=_params(1),
        name="inproj",
    )(x, gain, w_all, brow, sp_norm)


def _sb_chains(chains, upper):
    depth = max(len(tiles) for _, tiles in chains)
    order = [(c, p) for p in range(depth) for c, (_, tiles) in enumerate(chains) if p < len(tiles)]
    carries = [carry for carry, _ in chains]
    totals = [None] * len(chains)
    zs, mids = {}, {}
    for step in range(len(order) + 2 * SB_STAGE_LAG):
        if step < len(order):
            c, p = order[step]
            k, q, _, _ = chains[c][1][p]
            zs[c, p] = jnp.minimum(_dot(k, q), MAX_LOGIT2)
        if 0 <= step - SB_STAGE_LAG < len(order):
            c, p = order[step - SB_STAGE_LAG]
            valid = chains[c][1][p][3]
            z = zs.pop((c, p))
            sp = _softplus2(z)
            if valid is not None:
                sp = jnp.where(valid, sp, 0.0)
            mids[c, p] = (z, _dot(upper, sp.astype(BF16)))
        if 0 <= step - 2 * SB_STAGE_LAG < len(order):
            c, p = order[step - 2 * SB_STAGE_LAG]
            _, _, vT, valid = chains[c][1][p]
            z, newer = mids.pop((c, p))
            a = jnp.exp2(z - newer - carries[c])
            if valid is not None:
                a = jnp.where(valid, a, 0.0)
            o = _dot(vT, a.astype(BF16))
            totals[c] = o if totals[c] is None else totals[c] + o
            carries[c] = carries[c] + newer[0:1, :]
    return list(zip(totals, carries))


def _sbp_kernel(qT_ref, ka_ref, vT_ref, o_ref, acc_ref):
    t = ROW_BLOCK
    qi = pl.program_id(1)
    rk = lax.broadcasted_iota(jnp.int32, (t, t), 0)
    cq = lax.broadcasted_iota(jnp.int32, (t, t), 1)
    upper = jnp.where(cq >= rk, 1.0, 0.0).astype(BF16)
    causal = rk < cq
    zero = jnp.zeros((1, t), F32)
    streams = [(hh, half) for hh in range(SB_HEADS) for half in range(2)]

    def tile(kb, hh, half, valid):
        r0 = pl.multiple_of(kb * t, t)
        return (ka_ref[pl.ds(r0, t), hh * AUG:(hh + 1) * AUG],
                qT_ref[half, hh * AUG:(hh + 1) * AUG, :],
                vT_ref[kb, hh * D_SB:(hh + 1) * D_SB, :], valid)

    def acc_at(hh, half):
        return (slice(hh * D_SB, (hh + 1) * D_SB), slice(half * t, (half + 1) * t))

    hi, lo = 2 * qi + 1, 2 * qi
    chains = []
    for hh, half in streams:
        if half == 0:
            chains.append((zero, [tile(lo, hh, 0, causal)]))
        else:
            chains.append((zero, [tile(hi, hh, 1, causal), tile(lo, hh, 1, None)]))
    carries = []
    for (hh, half), (o, carry) in zip(streams, _sb_chains(chains, upper)):
        acc_ref[acc_at(hh, half)] = o
        carries.append(carry)

    def body(jj, carries):
        kb0 = lo - 1 - jj * SB_KEY_TILES
        chains = [(carries[i], [tile(kb0 - d, hh, half, None) for d in range(SB_KEY_TILES)])
                  for i, (hh, half) in enumerate(streams)]
        out = []
        for (hh, half), (o, carry) in zip(streams, _sb_chains(chains, upper)):
            acc_ref[acc_at(hh, half)] += o
            out.append(carry)
        return tuple(out)

    lax.fori_loop(0, lo // SB_KEY_TILES, body, tuple(carries))
    o_ref[...] = acc_ref[...].T


def _sb_prompt(qT, ka, vT):
    nb = qT.shape[0]
    t = ROW_BLOCK
    rows = nb * t
    return pl.pallas_call(
        _sbp_kernel,
        grid=(H_SB // SB_HEADS, nb // 2),
        in_specs=[
            pl.BlockSpec((2, SB_HEADS * AUG, t), lambda hg, qi: (qi, hg, 0)),
            _resident((rows, SB_HEADS * AUG), lambda hg, qi: (0, hg)),
            _resident((nb, SB_HEADS * D_SB, t), lambda hg, qi: (0, hg, 0)),
        ],
        out_specs=pl.BlockSpec((2 * t, SB_HEADS * D_SB), lambda hg, qi: (qi, hg)),
        out_shape=jax.ShapeDtypeStruct((rows, W_SB), F32),
        scratch_shapes=[pltpu.VMEM((SB_HEADS * D_SB, 2 * t), F32)],
        compiler_params=_params(2),
        name="sb_prompt",
    )(qT, ka, vT)


def _sbs_kernel(pt_ref, qbd_ref, bias_ref, kn_ref, vn_ref, ck_hbm, cv_hbm, o_ref,
                kbuf, vbuf, sems, *, layer, n_seq, n_pages):
    g_pages = PAGES_PER_STEP
    n_groups = n_pages // g_pages
    b = pl.program_id(0)
    nq = H_SB * SUBLANE
    qbd = qbd_ref[...]
    bias = bias_ref[...]
    rj = lax.broadcasted_iota(jnp.int32, (PAGE_SIZE, 2 * PAGE_SIZE), 0)
    cs = lax.broadcasted_iota(jnp.int32, (PAGE_SIZE, 2 * PAGE_SIZE), 1)
    newer_tot = jnp.where((rj > cs) | (cs >= PAGE_SIZE), 1.0, 0.0).astype(BF16)
    nt_dims = (((1,), (1,)), ((), ()))

    def group_copies(seq, group, slot):
        copies = []
        for i in range(g_pages):
            page = pt_ref[seq, n_pages - 1 - (group * g_pages + i)]
            lanes = pl.ds(i * PAGE_SIZE, PAGE_SIZE)
            copies.append(pltpu.make_async_copy(
                ck_hbm.at[layer, page], kbuf.at[slot, :, lanes], sems.at[0, slot, i]))
            copies.append(pltpu.make_async_copy(
                cv_hbm.at[layer, page], vbuf.at[slot, :, lanes], sems.at[1, slot, i]))
        return copies

    def start_group(seq, group, slot):
        for c in group_copies(seq, group, slot):
            c.start()

    def wait_group(seq, group, slot):
        for c in group_copies(seq, group, slot):
            c.wait()

    def attend(z, carry, valid, av):
        z = jnp.minimum(z, MAX_LOGIT2)
        sp = _softplus2(z)
        log_beta = z - sp
        if valid is not None:
            sp = jnp.where(valid, sp, 0.0)
        spb = sp.astype(BF16)
        n = z.shape[1] // PAGE_SIZE
        nts = [_dot(spb[:, i * PAGE_SIZE:(i + 1) * PAGE_SIZE], newer_tot) for i in range(n)]
        parts = []
        for i, nt in enumerate(nts):
            lanes = slice(i * PAGE_SIZE, (i + 1) * PAGE_SIZE)
            parts.append(jnp.exp2(log_beta[:, lanes] - nt[:, :PAGE_SIZE] - carry))
            carry = carry + nt[:, PAGE_SIZE:]
        a = parts[0] if n == 1 else jnp.concatenate(parts, axis=1)
        if valid is not None:
            a = jnp.where(valid, a, 0.0)
        return av(a.astype(BF16)), carry

    @pl.when(b == 0)
    def _():
        start_group(0, 0, 0)

    pad = jnp.zeros((PAGE_SIZE - SUBLANE, W_SB), F32)
    kn = jnp.concatenate([kn_ref[...], pad], axis=0).astype(BF16)
    vn = jnp.concatenate([vn_ref[...], pad], axis=0).astype(BF16)
    tq = lax.broadcasted_iota(jnp.int32, (nq, PAGE_SIZE), 0) % SUBLANE
    kj = lax.broadcasted_iota(jnp.int32, (nq, PAGE_SIZE), 1)
    z = lax.dot_general(qbd, kn, nt_dims, preferred_element_type=F32) + bias
    acc, carry = attend(z, jnp.zeros((nq, PAGE_SIZE), F32), kj < tq, lambda a: _dot(a, vn))

    bias_wide = jnp.concatenate([bias] * g_pages, axis=1)

    def body(group, state):
        acc, carry = state
        slot = group % 2

        @pl.when(group + 1 < n_groups)
        def _():
            start_group(b, group + 1, 1 - slot)

        @pl.when((group + 1 == n_groups) & (b + 1 < n_seq))
        def _():
            start_group(b + 1, 0, 0)

        wait_group(b, group, slot)
        kb = kbuf[slot].astype(BF16)
        vb = vbuf[slot].astype(BF16)
        z = _dot(qbd, kb) + bias_wide
        o, carry = attend(z, carry, None,
                          lambda a: lax.dot_general(a, vb, nt_dims, preferred_element_type=F32))
        return acc + o, carry

    acc, _ = lax.fori_loop(0, n_groups, body, (acc, carry))

    rh = lax.broadcasted_iota(jnp.int32, acc.shape, 0) // SUBLANE
    ch = lax.broadcasted_iota(jnp.int32, acc.shape, 1) // D_SB
    own = jnp.where(rh == ch, acc, 0.0)
    out = own[0:SUBLANE, :]
    for h in range(1, H_SB):
        out = out + own[h * SUBLANE:(h + 1) * SUBLANE, :]
    o_ref[...] = out


def _sb_sample(layer, page_table, qbd, bias_full, k_new, v_new, cache_k, cache_v):
    n_seq, n_pages = page_table.shape
    g_pages = PAGES_PER_STEP
    assert (n_pages // g_pages) % 2 == 0
    nq = H_SB * SUBLANE
    grid_spec = pltpu.PrefetchScalarGridSpec(
        num_scalar_prefetch=1,
        grid=(n_seq,),
        in_specs=[
            pl.BlockSpec((None, nq, W_SB), lambda b, pt: (b, 0, 0)),
            pl.BlockSpec((nq, PAGE_SIZE), lambda b, pt: (0, 0)),
            pl.BlockSpec((SUBLANE, W_SB), lambda b, pt: (b, 0)),
            pl.BlockSpec((SUBLANE, W_SB), lambda b, pt: (b, 0)),
            pl.BlockSpec(memory_space=pl.ANY),
            pl.BlockSpec(memory_space=pl.ANY),
        ],
        out_specs=pl.BlockSpec((SUBLANE, W_SB), lambda b, pt: (b, 0)),
        scratch_shapes=[
            pltpu.VMEM((2, W_SB, g_pages * PAGE_SIZE), F32),
            pltpu.VMEM((2, W_SB, g_pages * PAGE_SIZE), F32),
            pltpu.SemaphoreType.DMA((2, 2, g_pages)),
        ],
    )
    return pl.pallas_call(
        functools.partial(_sbs_kernel, layer=layer, n_seq=n_seq, n_pages=n_pages),
        grid_spec=grid_spec,
        out_shape=jax.ShapeDtypeStruct((n_seq * SUBLANE, W_SB), F32),
        compiler_params=_params(1),
        name="sb_sample",
    )(page_table, qbd, bias_full, k_new, v_new, cache_k, cache_v)


def _memattn_kernel(q_ref, mk_ref, mv_ref, o_ref):
    for h in range(H_MEM):
        cols = slice(h * D_MEM, (h + 1) * D_MEM)
        q = q_ref[:, cols].astype(BF16)
        s = lax.dot_general(q, mk_ref[:, cols].astype(BF16), (((1,), (1,)), ((), ())),
                            preferred_element_type=F32)
        e = jnp.exp(s - jnp.max(s, axis=-1, keepdims=True))
        p = e / jnp.sum(e, axis=-1, keepdims=True)
        o_ref[:, cols] = _dot(p.astype(BF16), mv_ref[:, cols].astype(BF16))


def _mem_attend(q, mk, mv, mk_at, mv_at, n_sets, tq):
    nrb = q.shape[0] // (n_sets * tq)
    return pl.pallas_call(
        _memattn_kernel,
        grid=(n_sets, nrb),
        in_specs=[
            pl.BlockSpec((tq, W_MEM), lambda b, r: (b * nrb + r, 0)),
            pl.BlockSpec((None, N_MEM, W_MEM), lambda b, r: mk_at(b)),
            pl.BlockSpec((None, N_MEM, W_MEM), lambda b, r: mv_at(b)),
        ],
        out_specs=pl.BlockSpec((tq, W_MEM), lambda b, r: (b * nrb + r, 0)),
        out_shape=jax.ShapeDtypeStruct(q.shape, F32),
        compiler_params=_params(2),
        name="mem_attend",
    )(q, mk, mv)


def _mix_kernel(h_ref, osb_ref, omem_ref, ug_ref, vn_ref, gate_ref, wt_ref, bt_ref,
                wbr_ref, wo_ref, out_ref, *, chunk):
    tm = h_ref.shape[0]
    shift = chunk.bit_length() - 1
    r = lax.broadcasted_iota(jnp.int32, (tm, tm), 0)
    c = lax.broadcasted_iota(jnp.int32, (tm, tm), 1)
    causal = (c <= r) & ((r >> shift) == (c >> shift))
    lane = lax.broadcasted_iota(jnp.int32, (tm, LANE), 1)
    gw = W_SP // SP_GROUPS
    vnb = vn_ref[...].astype(BF16)
    tiles = []
    for j in range(W_SP // LANE):
        vt = vnb[:, j * LANE:(j + 1) * LANE]
        per_group = []
        for g in range(LANE // gw):
            w = jnp.where(causal, wt_ref[j * (LANE // gw) + g], 0.0).astype(BF16)
            per_group.append(_dot(w, vt))
        tiles.append(jnp.where(lane < gw, per_group[0], per_group[1]))
    s = jnp.concatenate(tiles, axis=1) + bt_ref[...]
    osp = ug_ref[...].astype(F32) * s

    mixed = None
    for i, o in enumerate((osb_ref[...], osp, omem_ref[...])):
        br = _dot(o.astype(BF16), wbr_ref[i]) * gate_ref[:, i * D_MODEL:(i + 1) * D_MODEL]
        mixed = br if mixed is None else mixed + br
    out_ref[...] = h_ref[...] + _dot(mixed.astype(BF16), wo_ref[...])


def _mix(h, osb, omem, ug, vn, gates, wt, bt, wbr, wo, chunk):
    rows = h.shape[0]
    tm = ROW_BLOCK
    row = lambda i: (i, 0)
    return pl.pallas_call(
        functools.partial(_mix_kernel, chunk=chunk),
        grid=(rows // tm,),
        in_specs=[
            pl.BlockSpec((tm, D_MODEL), row),
            pl.BlockSpec((tm, W_SB), row),
            pl.BlockSpec((tm, W_MEM), row),
            pl.BlockSpec((tm, W_SP), row),
            pl.BlockSpec((tm, W_SP), row),
            pl.BlockSpec((tm, N_BRANCH * D_MODEL), row),
            _resident((SP_GROUPS, tm, tm), lambda i: (0, 0, 0)),
            _resident((tm, W_SP), lambda i: (0, 0)),
            _resident((N_BRANCH, W_SB, D_MODEL), lambda i: (0, 0, 0)),
            _resident((D_MODEL, D_MODEL), lambda i: (0, 0)),
        ],
        out_specs=pl.BlockSpec((tm, D_MODEL), row),
        out_shape=jax.ShapeDtypeStruct((rows, D_MODEL), F32),
        compiler_params=_params(1),
        name="mix",
    )(h, osb, omem, ug, vn, gates, wt, bt, wbr, wo)


def _ffn_kernel(*refs, seq_len, final_norm):
    h_ref, g_ref, wup_ref, cw_ref, cb_ref, wdn_ref, gf_ref = refs[:7]
    if seq_len is None:
        out_ref, up_ref, act_ref, carry_ref = refs[7:]
    else:
        s1_ref, s2_ref = refs[7:9]
        out_ref, up_ref, act_ref = refs[9:]
    tm = h_ref.shape[0]
    i = pl.program_id(0)
    h = h_ref[...]
    f = _rms(h, g_ref[...]).astype(BF16)
    pos = lax.broadcasted_iota(jnp.int32, (tm, FFN_COLS), 0)
    if seq_len is None:
        @pl.when(i == 0)
        def _():
            carry_ref[...] = jnp.zeros_like(carry_ref)
    else:
        pos = pos % seq_len

    for c in range(D_FF // FFN_COLS):
        conv = []
        for part in range(2):
            cols = slice(part * D_FF + c * FFN_COLS, part * D_FF + (c + 1) * FFN_COLS)
            up = _dot(f, wup_ref[:, cols])
            if seq_len is None:
                prev = carry_ref[:, cols]
                p1 = jnp.broadcast_to(prev[SUBLANE - 1:SUBLANE, :], up.shape)
                p2 = jnp.where(pos == 0, prev[SUBLANE - 2:SUBLANE - 1, :], p1)
                carry_ref[:, cols] = up[tm - SUBLANE:, :]
                up_ref[:, cols] = up[tm - SUBLANE:, :]
            else:
                p1 = s1_ref[:, cols]
                p2 = s2_ref[:, cols]
                up_ref[:, cols] = up
            x1 = jnp.where(pos >= 1, pltpu.roll(up, 1, 0), p1)
            x2 = jnp.where(pos >= 2, pltpu.roll(up, 2, 0), p2)
            conv.append(cb_ref[:, cols] + cw_ref[0:1, cols] * x2 + cw_ref[1:2, cols] * x1
                        + cw_ref[2:3, cols] * up)
        act_ref[:, c * FFN_COLS:(c + 1) * FFN_COLS] = (_gelu(conv[0]) * conv[1]).astype(BF16)

    y = h + _dot(act_ref[...], wdn_ref[...])
    if final_norm:
        y = _rms(y, gf_ref[...])
    out_ref[...] = y


def _ffn(h, gain, wup, cw, cb, wdn, gfinal, prev_rows, seq_len, final_norm):
    rows = h.shape[0]
    tm = ROW_BLOCK
    row = lambda i: (i, 0)
    const = lambda i: (0, 0)
    in_specs = [
        pl.BlockSpec((tm, D_MODEL), row),
        _resident((1, D_MODEL), const),
        _resident((D_MODEL, 2 * D_FF), const),
        _resident((CONV_W, 2 * D_FF), const),
        _resident((1, 2 * D_FF), const),
        _resident((D_FF, D_MODEL), const),
        _resident((1, D_MODEL), const),
    ]
    args = [h, gain, wup, cw, cb, wdn, gfinal]
    scratch = [pltpu.VMEM((tm, D_FF), BF16)]
    if seq_len is None:
        up_rows = SUBLANE
        up_spec = pl.BlockSpec((SUBLANE, 2 * D_FF), const)
        scratch.append(pltpu.VMEM((SUBLANE, 2 * D_FF), F32))
    else:
        up_rows = rows
        up_spec = pl.BlockSpec((tm, 2 * D_FF), row)
        in_specs += [_resident((tm, 2 * D_FF), row)] * 2
        args += list(prev_rows)
    return pl.pallas_call(
        functools.partial(_ffn_kernel, seq_len=seq_len, final_norm=final_norm),
        grid=(rows // tm,),
        in_specs=in_specs,
        out_specs=(pl.BlockSpec((tm, D_MODEL), row), up_spec),
        out_shape=(jax.ShapeDtypeStruct((rows, D_MODEL), F32),
                   jax.ShapeDtypeStruct((up_rows, 2 * D_FF), F32)),
        scratch_shapes=scratch,
        compiler_params=_params(1),
        name="ffn",
    )(*args)


def _augment(w, scale):
    w = (w * scale).reshape(D_MODEL, H_SB, D_SB)
    return jnp.pad(w, ((0, 0), (0, 0), (0, AUG - D_SB))).reshape(D_MODEL, H_SB * AUG)


def _fused_in_weight(w_in_l):
    sizes = [W_SB, W_SB, W_SB, W_SP, W_SP, W_MEM, N_BRANCH * D_MODEL]
    offs = [0]
    for s in sizes:
        offs.append(offs[-1] + s)
    wq, wk, wv, wu, wvs, wqm, wg = [w_in_l[:, offs[i]:offs[i + 1]] for i in range(7)]
    parts = [_augment(wq, LOG2E * D_SB ** -0.5), _augment(wk, 1.0), wk, wv, wu, wvs,
             wqm * D_MEM ** -0.5, wg]
    return jnp.concatenate(parts, axis=1).astype(BF16)


def _bias_row(sb_bias_l):
    b2 = sb_bias_l.astype(F32) * LOG2E
    hi = b2.astype(BF16).astype(F32)
    lo = (b2 - hi).astype(BF16).astype(F32)
    zq = jnp.zeros((H_SB, AUG), F32).at[:, D_SB].set(1.0).at[:, D_SB + 1].set(1.0)
    zk = jnp.zeros((H_SB, AUG), F32).at[:, D_SB].set(hi).at[:, D_SB + 1].set(lo)
    return jnp.concatenate([zq.reshape(1, -1), zk.reshape(1, -1)], axis=1)


def _spatial_operands(sp_w_l, sp_b_l, chunk, tm):
    reps = tm // chunk
    wt = jnp.tile(sp_w_l[:, :chunk, :chunk], (1, reps, reps))
    bt = jnp.tile(jnp.repeat(sp_b_l[:, :chunk].T, W_SP // SP_GROUPS, axis=1), (reps, 1))
    return wt, bt


def kernel(x_prompt, x_sample, mem_prompt, cache_sb_k, cache_sb_v, page_table, cache_mem_k,
           cache_mem_v, state_ffn_conv, norm_mix, w_in, sb_bias, sp_norm, sp_w, sp_b, mem_norm,
           w_mem_kv, w_branch, w_o, norm_ffn, w_up, conv_w, conv_b, w_down, norm_final):
    depth = w_in.shape[0]
    batch, seq, _ = x_prompt.shape
    n_seq, dec_seq, _ = x_sample.shape
    assert batch == 1 and dec_seq == SUBLANE and n_seq * dec_seq == ROW_BLOCK
    assert seq % ROW_BLOCK == 0 and page_table.shape[1] % PAGES_PER_STEP == 0

    hp = x_prompt.reshape(seq, D_MODEL)
    hs = x_sample.reshape(n_seq * dec_seq, D_MODEL)
    n_pool = cache_sb_k.shape[1]
    cache_k = jnp.transpose(cache_sb_k, (0, 1, 3, 4, 2)).reshape(depth, n_pool, W_SB, PAGE_SIZE)
    cache_v = jnp.transpose(cache_sb_v, (0, 1, 3, 4, 2)).reshape(depth, n_pool, W_SB, PAGE_SIZE)
    mem_kv = _memkv(mem_prompt.reshape(N_MEM, D_MODEL), mem_norm, w_mem_kv.astype(BF16))
    mem_k_s = cache_mem_k.reshape(depth * n_seq, N_MEM, W_MEM)
    mem_v_s = cache_mem_v.reshape(depth * n_seq, N_MEM, W_MEM)
    gfinal = norm_final.reshape(1, D_MODEL)
    head_eye = jnp.eye(H_SB, dtype=BF16)

    pk, pv, sk, sv, spv, pconv, sconv = [], [], [], [], [], [], []
    for l in range(depth):
        w_all = _fused_in_weight(w_in[l])
        brow = _bias_row(sb_bias[l])
        gain_mix = norm_mix[l].reshape(1, D_MODEL)
        spn = sp_norm[l].reshape(1, W_SP)
        wbr = w_branch[l].astype(BF16)
        wo = w_o[l].astype(BF16)
        gain_ffn = norm_ffn[l].reshape(1, D_MODEL)
        wup = w_up[l].astype(BF16)
        wdn = w_down[l].astype(BF16)
        cw = conv_w[l]
        cb = conv_b[l].reshape(1, 2 * D_FF)
        last = l == depth - 1

        qT, ka, k_p, v_p, vT, ug, vn, qm, gates = _inproj(hp, gain_mix, w_all, brow, spn, True)
        osb = _sb_prompt(qT, ka, vT)
        omem = _mem_attend(qm, mem_kv, mem_kv, lambda b, l=l: (l, 0, 0), lambda b, l=l: (l, 0, 1),
                           1, 2 * ROW_BLOCK)
        wt, bt = _spatial_operands(sp_w[l], sp_b[l], CHUNK, ROW_BLOCK)
        hp = _mix(hp, osb, omem, ug, vn, gates, wt, bt, wbr, wo, CHUNK)
        hp, up_last = _ffn(hp, gain_ffn, wup, cw, cb, wdn, gfinal, None, None, last)
        pk.append(k_p)
        pv.append(v_p)
        pconv.append(up_last[SUBLANE - (CONV_W - 1):, :])

        qT, ka, k_s, v_s, vT, ug, vn, qm, gates = _inproj(hs, gain_mix, w_all, brow, spn, False)
        q = qT.reshape(H_SB, AUG, n_seq, dec_seq)[:, :D_SB]
        qbd = jnp.einsum('hdbt,hg->bhtgd', q, head_eye).reshape(n_seq, H_SB * dec_seq, W_SB)
        bias_full = jnp.broadcast_to(
            jnp.repeat(sb_bias[l].astype(F32) * LOG2E, dec_seq)[:, None], (H_SB * dec_seq, PAGE_SIZE))
        osb = _sb_sample(l, page_table, qbd, bias_full, k_s, v_s, cache_k, cache_v)
        at = lambda b, l=l: (l * n_seq + b, 0, 0)
        omem = _mem_attend(qm, mem_k_s, mem_v_s, at, at, n_seq, dec_seq)
        wt, bt = _spatial_operands(sp_w[l], sp_b[l], dec_seq, ROW_BLOCK)
        hs = _mix(hs, osb, omem, ug, vn, gates, wt, bt, wbr, wo, dec_seq)
        st = state_ffn_conv[l]
        s1 = jnp.repeat(st[:, 1:2, :], dec_seq, axis=1).reshape(n_seq * dec_seq, 2 * D_FF)
        s2 = jnp.tile(st, (1, dec_seq // (CONV_W - 1), 1)).reshape(n_seq * dec_seq, 2 * D_FF)
        hs, up_s = _ffn(hs, gain_ffn, wup, cw, cb, wdn, gfinal, (s1, s2), dec_seq, last)
        sk.append(k_s)
        sv.append(v_s)
        spv.append(vn)
        sconv.append(up_s.reshape(n_seq, dec_seq, 2 * D_FF)[:, dec_seq - (CONV_W - 1):, :])

    n_pp = seq // PAGE_SIZE
    y_prompt = hp.reshape(batch, seq, D_MODEL)
    y_sample = hs.reshape(n_seq, dec_seq, D_MODEL)
    return (
        y_prompt,
        y_sample,
        jnp.transpose(jnp.stack(pk).reshape(depth, n_pp, H_SB, D_SB, PAGE_SIZE), (0, 1, 4, 2, 3)),
        jnp.transpose(jnp.stack(pv).reshape(depth, n_pp, H_SB, D_SB, PAGE_SIZE), (0, 1, 4, 2, 3)),
        jnp.stack(sk).reshape(depth, n_seq, dec_seq, H_SB, D_SB),
        jnp.stack(sv).reshape(depth, n_seq, dec_seq, H_SB, D_SB),
        mem_kv[:, :, :W_MEM].reshape(depth, batch, N_MEM, H_MEM, D_MEM),
        mem_kv[:, :, W_MEM:].reshape(depth, batch, N_MEM, H_MEM, D_MEM),
        jnp.stack(spv).reshape(depth, n_seq, dec_seq, W_SP),
        jnp.stack(pconv).reshape(depth, batch, CONV_W - 1, 2 * D_FF),
        jnp.stack(sconv).reshape(depth, n_seq, CONV_W - 1, 2 * D_FF),
    )
```

```python
import functools
import math

import jax
import jax.numpy as jnp
from jax import lax
from jax.experimental import pallas as pl
from jax.experimental.pallas import tpu as pltpu

F32 = jnp.float32
BF16 = jnp.bfloat16

D_MODEL = 1024
H_SB = 8
D_SB = 64
W_SB = H_SB * D_SB
SP_GROUPS = 8
W_SP = 512
CHUNK = 128
N_MEM = 256
H_MEM = 4
D_MEM = 128
W_MEM = H_MEM * D_MEM
N_BRANCH = 3
D_FF = 2816
CONV_W = 3
EPS = 1e-6
PAGE_SIZE = 128

LOG2E = 1.4426950408889634
MAX_LOGIT2 = 126.0
LANE = 128
SUBLANE = 8
ROW_BLOCK = 256
AUG = 2 * D_SB
FFN_COLS = 256
SB_HEADS = 4
SB_Q_TILES = 4
SB_KEY_TILES = 2
SB_STAGE_LAG = 2
SBS_SLOTS = 3
PAGES_PER_STEP = 8
VMEM_LIMIT = 56 * 1024 * 1024

C_QA = 0
C_KA = C_QA + H_SB * AUG
C_K = C_KA + H_SB * AUG
C_V = C_K + W_SB
C_U = C_V + W_SB
C_VS = C_U + W_SP
C_QM = C_VS + W_SP
C_G = C_QM + W_MEM
C_END = C_G + N_BRANCH * D_MODEL


def _params(n_axes, limit=VMEM_LIMIT):
    return pltpu.CompilerParams(
        dimension_semantics=("arbitrary",) * n_axes, vmem_limit_bytes=limit)


def _resident(shape, index_map):
    return pl.BlockSpec(shape, index_map, pipeline_mode=pl.Buffered(1))


def _rms(x, gain):
    return x * lax.rsqrt(jnp.mean(x * x, axis=-1, keepdims=True) + EPS) * gain


def _gelu(x):
    c = math.sqrt(2.0 / math.pi)
    return x * (0.5 * (1.0 + jnp.tanh(c * (x + 0.044715 * (x * x * x)))))


def _dot(a, b):
    return jnp.dot(a, b, preferred_element_type=F32)


def _softplus2(z):
    return jnp.log2(1.0 + jnp.exp2(z))


def _memkv_kernel(mem_ref, g_ref, w_ref, kv_ref):
    a = _rms(mem_ref[...], g_ref[...]).astype(BF16)
    kv_ref[...] = _dot(a, w_ref[...])


def _memkv(mem, mem_norm, w_mem_kv_bf):
    depth = w_mem_kv_bf.shape[0]
    return pl.pallas_call(
        _memkv_kernel,
        grid=(depth,),
        in_specs=[
            pl.BlockSpec((N_MEM, D_MODEL), lambda l: (0, 0)),
            pl.BlockSpec((None, 1, D_MODEL), lambda l: (l, 0, 0)),
            pl.BlockSpec((None, D_MODEL, 2 * W_MEM), lambda l: (l, 0, 0)),
        ],
        out_specs=pl.BlockSpec((None, N_MEM, 2 * W_MEM), lambda l: (l, 0, 0)),
        out_shape=jax.ShapeDtypeStruct((depth, N_MEM, 2 * W_MEM), F32),
        compiler_params=_params(1),
        name="memkv",
    )(mem, mem_norm.reshape(depth, 1, D_MODEL), w_mem_kv_bf)


def _inproj_kernel(x_ref, g_ref, w_ref, brow_ref, spn_ref,
                   qT_ref, ka_ref, k_ref, v_ref, vT_ref, ug_ref, vn_ref, qm_ref, gate_ref, *, paged):
    a = _rms(x_ref[...], g_ref[...]).astype(BF16)

    def mm(c0, c1):
        return _dot(a, w_ref[:, c0:c1])

    qa = mm(C_QA, C_KA) + brow_ref[:, :H_SB * AUG]
    qT_ref[...] = qa.T.astype(BF16)
    ka = mm(C_KA, C_K) + brow_ref[:, H_SB * AUG:]
    ka_ref[...] = ka.astype(BF16)
    k = mm(C_K, C_V)
    v = mm(C_V, C_U)
    vT = v.T
    vT_ref[...] = vT.astype(BF16)
    if paged:
        kT = k.T
        for j in range(k_ref.shape[0]):
            k_ref[j] = kT[:, j * PAGE_SIZE:(j + 1) * PAGE_SIZE]
            v_ref[j] = vT[:, j * PAGE_SIZE:(j + 1) * PAGE_SIZE]
    else:
        k_ref[...] = k
        v_ref[...] = v
    ug_ref[...] = _gelu(mm(C_U, C_VS)).astype(BF16)
    vn_ref[...] = _rms(_gelu(mm(C_VS, C_QM)), spn_ref[...])
    qm_ref[...] = mm(C_QM, C_G)
    for i in range(N_BRANCH):
        g = mm(C_G + i * D_MODEL, C_G + (i + 1) * D_MODEL)
        gate_ref[:, i * D_MODEL:(i + 1) * D_MODEL] = jax.nn.sigmoid(g)


def _inproj(x, gain, w_all, brow, sp_norm, paged):
    rows = x.shape[0]
    tm = ROW_BLOCK
    nb = rows // tm
    row = lambda i: (i, 0)
    const = lambda i: (0, 0)
    if paged:
        ppb = tm // PAGE_SIZE
        kv_shape = jax.ShapeDtypeStruct((rows // PAGE_SIZE, W_SB, PAGE_SIZE), F32)
        kv_spec = pl.BlockSpec((ppb, W_SB, PAGE_SIZE), lambda i: (i, 0, 0))
    else:
        kv_shape = jax.ShapeDtypeStruct((rows, W_SB), F32)
        kv_spec = pl.BlockSpec((tm, W_SB), row)
    out_shapes = (
        jax.ShapeDtypeStruct((nb, H_SB * AUG, tm), BF16),
        jax.ShapeDtypeStruct((rows, H_SB * AUG), BF16),
        kv_shape,
        kv_shape,
        jax.ShapeDtypeStruct((nb, W_SB, tm), BF16),
        jax.ShapeDtypeStruct((rows, W_SP), BF16),
        jax.ShapeDtypeStruct((rows, W_SP), F32),
        jax.ShapeDtypeStruct((rows, W_MEM), F32),
        jax.ShapeDtypeStruct((rows, N_BRANCH * D_MODEL), F32),
    )
    out_specs = (
        pl.BlockSpec((None, H_SB * AUG, tm), lambda i: (i, 0, 0)),
        pl.BlockSpec((tm, H_SB * AUG), row),
        kv_spec,
        kv_spec,
        pl.BlockSpec((None, W_SB, tm), lambda i: (i, 0, 0)),
        pl.BlockSpec((tm, W_SP), row),
        pl.BlockSpec((tm, W_SP), row),
        pl.BlockSpec((tm, W_MEM), row),
        pl.BlockSpec((tm, N_BRANCH * D_MODEL), row),
    )
    return pl.pallas_call(
        functools.partial(_inproj_kernel, paged=paged),
        grid=(nb,),
        in_specs=[
            pl.BlockSpec((tm, D_MODEL), row),
            _resident((1, D_MODEL), const),
            _resident((D_MODEL, C_END), const),
            _resident((1, 2 * H_SB * AUG), const),
            _resident((1, W_SP), const),
        ],
        out_specs=out_specs,
        out_shape=out_shapes,
        compiler_params=_params(1),
        name="inproj",
    )(x, gain, w_all, brow, sp_norm)


def _sb_chains(chains, upper):
    depth = max(len(tiles) for _, tiles in chains)
    order = [(c, p) for p in range(depth) for c, (_, tiles) in enumerate(chains) if p < len(tiles)]
    carries = [carry for carry, _ in chains]
    totals = [None] * len(chains)
    zs, mids = {}, {}
    for step in range(len(order) + 2 * SB_STAGE_LAG):
        if step < len(order):
            c, p = order[step]
            k, q, _, _ = chains[c][1][p]
            zs[c, p] = jnp.minimum(_dot(k, q), MAX_LOGIT2)
        if 0 <= step - SB_STAGE_LAG < len(order):
            c, p = order[step - SB_STAGE_LAG]
            valid = chains[c][1][p][3]
            z = zs.pop((c, p))
            sp = _softplus2(z)
            if valid is not None:
                sp = jnp.where(valid, sp, 0.0)
            mids[c, p] = (z, _dot(upper, sp.astype(BF16)))
        if 0 <= step - 2 * SB_STAGE_LAG < len(order):
            c, p = order[step - 2 * SB_STAGE_LAG]
            _, _, vT, valid = chains[c][1][p]
            z, newer = mids.pop((c, p))
            a = jnp.exp2((z - newer - carries[c]).astype(BF16))
            if valid is not None:
                a = jnp.where(valid, a, jnp.zeros_like(a))
            o = _dot(vT, a)
            totals[c] = o if totals[c] is None else totals[c] + o
            carries[c] = carries[c] + newer[0:1, :]
    return list(zip(totals, carries))


def _sbp_kernel(qT_ref, ka_ref, vT_ref, o_ref, acc_ref):
    t = ROW_BLOCK
    qi = pl.program_id(1)
    rk = lax.broadcasted_iota(jnp.int32, (t, t), 0)
    cq = lax.broadcasted_iota(jnp.int32, (t, t), 1)
    upper = jnp.where(cq >= rk, 1.0, 0.0).astype(BF16)
    causal = rk < cq
    zero = jnp.zeros((1, t), F32)
    streams = [(hh, part) for hh in range(SB_HEADS) for part in range(SB_Q_TILES)]

    def tile(kb, hh, part, valid):
        r0 = pl.multiple_of(kb * t, t)
        return (ka_ref[pl.ds(r0, t), hh * AUG:(hh + 1) * AUG],
                qT_ref[part, hh * AUG:(hh + 1) * AUG, :],
                vT_ref[kb, hh * D_SB:(hh + 1) * D_SB, :], valid)

    def acc_at(hh, part):
        return (slice(hh * D_SB, (hh + 1) * D_SB), slice(part * t, (part + 1) * t))

    lo = SB_Q_TILES * qi
    chains = [(zero, [tile(lo + part, hh, part, causal)]
               + [tile(lo + part - 1 - d, hh, part, None) for d in range(part)])
              for hh, part in streams]
    carries = []
    for (hh, part), (o, carry) in zip(streams, _sb_chains(chains, upper)):
        acc_ref[acc_at(hh, part)] = o
        carries.append(carry)

    def body(jj, carries):
        kb0 = lo - 1 - jj * SB_KEY_TILES
        chains = [(carries[i], [tile(kb0 - d, hh, part, None) for d in range(SB_KEY_TILES)])
                  for i, (hh, part) in enumerate(streams)]
        out = []
        for (hh, part), (o, carry) in zip(streams, _sb_chains(chains, upper)):
            acc_ref[acc_at(hh, part)] += o
            out.append(carry)
        return tuple(out)

    lax.fori_loop(0, lo // SB_KEY_TILES, body, tuple(carries))
    o_ref[...] = acc_ref[...].T


def _sb_prompt(qT, ka, vT):
    nb = qT.shape[0]
    t = ROW_BLOCK
    rows = nb * t
    return pl.pallas_call(
        _sbp_kernel,
        grid=(H_SB // SB_HEADS, nb // SB_Q_TILES),
        in_specs=[
            pl.BlockSpec((SB_Q_TILES, SB_HEADS * AUG, t), lambda hg, qi: (qi, hg, 0)),
            _resident((rows, SB_HEADS * AUG), lambda hg, qi: (0, hg)),
            _resident((nb, SB_HEADS * D_SB, t), lambda hg, qi: (0, hg, 0)),
        ],
        out_specs=pl.BlockSpec((SB_Q_TILES * t, SB_HEADS * D_SB), lambda hg, qi: (qi, hg)),
        out_shape=jax.ShapeDtypeStruct((rows, W_SB), F32),
        scratch_shapes=[pltpu.VMEM((SB_HEADS * D_SB, SB_Q_TILES * t), F32)],
        compiler_params=_params(2),
        name="sb_prompt",
    )(qT, ka, vT)


def _sbs_kernel(pt_ref, qbd_ref, bias_ref, kn_ref, vn_ref, ck_hbm, cv_hbm, o_ref,
                kbuf, vbuf, sems, *, layer, n_seq, n_pages):
    g_pages = PAGES_PER_STEP
    n_groups = n_pages // g_pages
    b = pl.program_id(0)
    nq = H_SB * SUBLANE
    qbd = qbd_ref[...]
    bias = bias_ref[...]
    rj = lax.broadcasted_iota(jnp.int32, (PAGE_SIZE, 2 * PAGE_SIZE), 0)
    cs = lax.broadcasted_iota(jnp.int32, (PAGE_SIZE, 2 * PAGE_SIZE), 1)
    newer_tot = jnp.where((rj > cs) | (cs >= PAGE_SIZE), 1.0, 0.0).astype(BF16)
    nt_dims = (((1,), (1,)), ((), ()))

    def group_copies(gidx):
        seq = gidx // n_groups
        group = gidx % n_groups
        slot = gidx % SBS_SLOTS
        copies = []
        for i in range(g_pages):
            page = pt_ref[seq, n_pages - 1 - (group * g_pages + i)]
            lanes = pl.ds(i * PAGE_SIZE, PAGE_SIZE)
            copies.append(pltpu.make_async_copy(
                ck_hbm.at[layer, page], kbuf.at[slot, :, lanes], sems.at[0, slot, i]))
            copies.append(pltpu.make_async_copy(
                cv_hbm.at[layer, page], vbuf.at[slot, :, lanes], sems.at[1, slot, i]))
        return copies

    def start_group(gidx):
        for c in group_copies(gidx):
            c.start()

    def wait_group(gidx):
        for c in group_copies(gidx):
            c.wait()

    def attend(z, carry, valid, av):
        z = jnp.minimum(z, MAX_LOGIT2)
        sp = _softplus2(z)
        log_beta = z - sp
        if valid is not None:
            sp = jnp.where(valid, sp, 0.0)
        spb = sp.astype(BF16)
        n = z.shape[1] // PAGE_SIZE
        nts = [_dot(spb[:, i * PAGE_SIZE:(i + 1) * PAGE_SIZE], newer_tot) for i in range(n)]
        parts = []
        for i, nt in enumerate(nts):
            lanes = slice(i * PAGE_SIZE, (i + 1) * PAGE_SIZE)
            parts.append(jnp.exp2(log_beta[:, lanes] - nt[:, :PAGE_SIZE] - carry))
            carry = carry + nt[:, PAGE_SIZE:]
        a = parts[0] if n == 1 else jnp.concatenate(parts, axis=1)
        if valid is not None:
            a = jnp.where(valid, a, 0.0)
        return av(a.astype(BF16)), carry

    @pl.when(b == 0)
    def _():
        for gidx in range(SBS_SLOTS - 1):
            start_group(gidx)

    pad = jnp.zeros((PAGE_SIZE - SUBLANE, W_SB), F32)
    kn = jnp.concatenate([kn_ref[...], pad], axis=0).astype(BF16)
    vn = jnp.concatenate([vn_ref[...], pad], axis=0).astype(BF16)
    tq = lax.broadcasted_iota(jnp.int32, (nq, PAGE_SIZE), 0) % SUBLANE
    kj = lax.broadcasted_iota(jnp.int32, (nq, PAGE_SIZE), 1)
    z = lax.dot_general(qbd, kn, nt_dims, preferred_element_type=F32) + bias
    acc, carry = attend(z, jnp.zeros((nq, PAGE_SIZE), F32), kj < tq, lambda a: _dot(a, vn))

    bias_wide = jnp.concatenate([bias] * g_pages, axis=1)

    def body(group, state):
        acc, carry = state
        gidx = b * n_groups + group
        slot = gidx % SBS_SLOTS

        @pl.when(gidx + SBS_SLOTS - 1 < n_seq * n_groups)
        def _():
            start_group(gidx + SBS_SLOTS - 1)

        wait_group(gidx)
        kb = kbuf[slot].astype(BF16)
        vb = vbuf[slot].astype(BF16)
        z = _dot(qbd, kb) + bias_wide
        o, carry = attend(z, carry, None,
                          lambda a: lax.dot_general(a, vb, nt_dims, preferred_element_type=F32))
        return acc + o, carry

    acc, _ = lax.fori_loop(0, n_groups, body, (acc, carry))

    rh = lax.broadcasted_iota(jnp.int32, acc.shape, 0) // SUBLANE
    ch = lax.broadcasted_iota(jnp.int32, acc.shape, 1) // D_SB
    own = jnp.where(rh == ch, acc, 0.0)
    out = own[0:SUBLANE, :]
    for h in range(1, H_SB):
        out = out + own[h * SUBLANE:(h + 1) * SUBLANE, :]
    o_ref[...] = out


def _sb_sample(layer, page_table, qbd, bias_full, k_new, v_new, cache_k, cache_v):
    n_seq, n_pages = page_table.shape
    g_pages = PAGES_PER_STEP
    assert n_pages % g_pages == 0 and n_seq * (n_pages // g_pages) >= SBS_SLOTS - 1
    nq = H_SB * SUBLANE
    grid_spec = pltpu.PrefetchScalarGridSpec(
        num_scalar_prefetch=1,
        grid=(n_seq,),
        in_specs=[
            pl.BlockSpec((None, nq, W_SB), lambda b, pt: (b, 0, 0)),
            pl.BlockSpec((nq, PAGE_SIZE), lambda b, pt: (0, 0)),
            pl.BlockSpec((SUBLANE, W_SB), lambda b, pt: (b, 0)),
            pl.BlockSpec((SUBLANE, W_SB), lambda b, pt: (b, 0)),
            pl.BlockSpec(memory_space=pl.ANY),
            pl.BlockSpec(memory_space=pl.ANY),
        ],
        out_specs=pl.BlockSpec((SUBLANE, W_SB), lambda b, pt: (b, 0)),
        scratch_shapes=[
            pltpu.VMEM((SBS_SLOTS, W_SB, g_pages * PAGE_SIZE), F32),
            pltpu.VMEM((SBS_SLOTS, W_SB, g_pages * PAGE_SIZE), F32),
            pltpu.SemaphoreType.DMA((2, SBS_SLOTS, g_pages)),
        ],
    )
    return pl.pallas_call(
        functools.partial(_sbs_kernel, layer=layer, n_seq=n_seq, n_pages=n_pages),
        grid_spec=grid_spec,
        out_shape=jax.ShapeDtypeStruct((n_seq * SUBLANE, W_SB), F32),
        compiler_params=_params(1),
        name="sb_sample",
    )(page_table, qbd, bias_full, k_new, v_new, cache_k, cache_v)


def _memattn_kernel(q_ref, mk_ref, mv_ref, o_ref):
    for h in range(H_MEM):
        cols = slice(h * D_MEM, (h + 1) * D_MEM)
        q = q_ref[:, cols].astype(BF16)
        s = lax.dot_general(q, mk_ref[:, cols].astype(BF16), (((1,), (1,)), ((), ())),
                            preferred_element_type=F32)
        e = jnp.exp(s - jnp.max(s, axis=-1, keepdims=True))
        p = e / jnp.sum(e, axis=-1, keepdims=True)
        o_ref[:, cols] = _dot(p.astype(BF16), mv_ref[:, cols].astype(BF16))


def _mem_attend(q, mk, mv, mk_at, mv_at, n_sets, tq):
    nrb = q.shape[0] // (n_sets * tq)
    return pl.pallas_call(
        _memattn_kernel,
        grid=(n_sets, nrb),
        in_specs=[
            pl.BlockSpec((tq, W_MEM), lambda b, r: (b * nrb + r, 0)),
            pl.BlockSpec((None, N_MEM, W_MEM), lambda b, r: mk_at(b)),
            pl.BlockSpec((None, N_MEM, W_MEM), lambda b, r: mv_at(b)),
        ],
        out_specs=pl.BlockSpec((tq, W_MEM), lambda b, r: (b * nrb + r, 0)),
        out_shape=jax.ShapeDtypeStruct(q.shape, F32),
        compiler_params=_params(2),
        name="mem_attend",
    )(q, mk, mv)


def _mix_kernel(h_ref, osb_ref, omem_ref, ug_ref, vn_ref, gate_ref, wt_ref, bt_ref,
                wbr_ref, wo_ref, out_ref, *, chunk):
    tm = h_ref.shape[0]
    shift = chunk.bit_length() - 1
    r = lax.broadcasted_iota(jnp.int32, (tm, tm), 0)
    c = lax.broadcasted_iota(jnp.int32, (tm, tm), 1)
    causal = (c <= r) & ((r >> shift) == (c >> shift))
    lane = lax.broadcasted_iota(jnp.int32, (tm, LANE), 1)
    gw = W_SP // SP_GROUPS
    vnb = vn_ref[...].astype(BF16)
    tiles = []
    for j in range(W_SP // LANE):
        vt = vnb[:, j * LANE:(j + 1) * LANE]
        per_group = []
        for g in range(LANE // gw):
            w = jnp.where(causal, wt_ref[j * (LANE // gw) + g], 0.0).astype(BF16)
            per_group.append(_dot(w, vt))
        tiles.append(jnp.where(lane < gw, per_group[0], per_group[1]))
    s = jnp.concatenate(tiles, axis=1) + bt_ref[...]
    osp = ug_ref[...].astype(F32) * s

    mixed = None
    for i, o in enumerate((osb_ref[...], osp, omem_ref[...])):
        br = _dot(o.astype(BF16), wbr_ref[i]) * gate_ref[:, i * D_MODEL:(i + 1) * D_MODEL]
        mixed = br if mixed is None else mixed + br
    out_ref[...] = h_ref[...] + _dot(mixed.astype(BF16), wo_ref[...])


def _mix(h, osb, omem, ug, vn, gates, wt, bt, wbr, wo, chunk):
    rows = h.shape[0]
    tm = ROW_BLOCK
    row = lambda i: (i, 0)
    return pl.pallas_call(
        functools.partial(_mix_kernel, chunk=chunk),
        grid=(rows // tm,),
        in_specs=[
            pl.BlockSpec((tm, D_MODEL), row),
            pl.BlockSpec((tm, W_SB), row),
            pl.BlockSpec((tm, W_MEM), row),
            pl.BlockSpec((tm, W_SP), row),
            pl.BlockSpec((tm, W_SP), row),
            pl.BlockSpec((tm, N_BRANCH * D_MODEL), row),
            _resident((SP_GROUPS, tm, tm), lambda i: (0, 0, 0)),
            _resident((tm, W_SP), lambda i: (0, 0)),
            _resident((N_BRANCH, W_SB, D_MODEL), lambda i: (0, 0, 0)),
            _resident((D_MODEL, D_MODEL), lambda i: (0, 0)),
        ],
        out_specs=pl.BlockSpec((tm, D_MODEL), row),
        out_shape=jax.ShapeDtypeStruct((rows, D_MODEL), F32),
        compiler_params=_params(1),
        name="mix",
    )(h, osb, omem, ug, vn, gates, wt, bt, wbr, wo)


def _ffn_kernel(*refs, seq_len, final_norm):
    h_ref, g_ref, wup_ref, cw_ref, cb_ref, wdn_ref, gf_ref = refs[:7]
    if seq_len is None:
        out_ref, up_ref, act_ref, carry_ref = refs[7:]
    else:
        s1_ref, s2_ref = refs[7:9]
        out_ref, up_ref, act_ref = refs[9:]
    tm = h_ref.shape[0]
    i = pl.program_id(0)
    h = h_ref[...]
    f = _rms(h, g_ref[...]).astype(BF16)
    pos = lax.broadcasted_iota(jnp.int32, (tm, FFN_COLS), 0)
    if seq_len is None:
        @pl.when(i == 0)
        def _():
            carry_ref[...] = jnp.zeros_like(carry_ref)
    else:
        pos = pos % seq_len

    for c in range(D_FF // FFN_COLS):
        conv = []
        for part in range(2):
            cols = slice(part * D_FF + c * FFN_COLS, part * D_FF + (c + 1) * FFN_COLS)
            up = _dot(f, wup_ref[:, cols])
            if seq_len is None:
                prev = carry_ref[:, cols]
                p1 = jnp.broadcast_to(prev[SUBLANE - 1:SUBLANE, :], up.shape)
                p2 = jnp.where(pos == 0, prev[SUBLANE - 2:SUBLANE - 1, :], p1)
                carry_ref[:, cols] = up[tm - SUBLANE:, :]
                up_ref[:, cols] = up[tm - SUBLANE:, :]
            else:
                p1 = s1_ref[:, cols]
                p2 = s2_ref[:, cols]
                up_ref[:, cols] = up
            x1 = jnp.where(pos >= 1, pltpu.roll(up, 1, 0), p1)
            x2 = jnp.where(pos >= 2, pltpu.roll(up, 2, 0), p2)
            conv.append(cb_ref[:, cols] + cw_ref[0:1, cols] * x2 + cw_ref[1:2, cols] * x1
                        + cw_ref[2:3, cols] * up)
        act_ref[:, c * FFN_COLS:(c + 1) * FFN_COLS] = (_gelu(conv[0]) * conv[1]).astype(BF16)

    y = h + _dot(act_ref[...], wdn_ref[...])
    if final_norm:
        y = _rms(y, gf_ref[...])
    out_ref[...] = y


def _ffn(h, gain, wup, cw, cb, wdn, gfinal, prev_rows, seq_len, final_norm):
    rows = h.shape[0]
    tm = ROW_BLOCK
    row = lambda i: (i, 0)
    const = lambda i: (0, 0)
    in_specs = [
        pl.BlockSpec((tm, D_MODEL), row),
        _resident((1, D_MODEL), const),
        _resident((D_MODEL, 2 * D_FF), const),
        _resident((CONV_W, 2 * D_FF), const),
        _resident((1, 2 * D_FF), const),
        _resident((D_FF, D_MODEL), const),
        _resident((1, D_MODEL), const),
    ]
    args = [h, gain, wup, cw, cb, wdn, gfinal]
    scratch = [pltpu.VMEM((tm, D_FF), BF16)]
    if seq_len is None:
        up_rows = SUBLANE
        up_spec = pl.BlockSpec((SUBLANE, 2 * D_FF), const)
        scratch.append(pltpu.VMEM((SUBLANE, 2 * D_FF), F32))
    else:
        up_rows = rows
        up_spec = pl.BlockSpec((tm, 2 * D_FF), row)
        in_specs += [_resident((tm, 2 * D_FF), row)] * 2
        args += list(prev_rows)
    return pl.pallas_call(
        functools.partial(_ffn_kernel, seq_len=seq_len, final_norm=final_norm),
        grid=(rows // tm,),
        in_specs=in_specs,
        out_specs=(pl.BlockSpec((tm, D_MODEL), row), up_spec),
        out_shape=(jax.ShapeDtypeStruct((rows, D_MODEL), F32),
                   jax.ShapeDtypeStruct((up_rows, 2 * D_FF), F32)),
        scratch_shapes=scratch,
        compiler_params=_params(1),
        name="ffn",
    )(*args)


def _augment(w, scale):
    w = (w * scale).reshape(D_MODEL, H_SB, D_SB)
    return jnp.pad(w, ((0, 0), (0, 0), (0, AUG - D_SB))).reshape(D_MODEL, H_SB * AUG)


def _fused_in_weight(w_in_l):
    sizes = [W_SB, W_SB, W_SB, W_SP, W_SP, W_MEM, N_BRANCH * D_MODEL]
    offs = [0]
    for s in sizes:
        offs.append(offs[-1] + s)
    wq, wk, wv, wu, wvs, wqm, wg = [w_in_l[:, offs[i]:offs[i + 1]] for i in range(7)]
    parts = [_augment(wq, LOG2E * D_SB ** -0.5), _augment(wk, 1.0), wk, wv, wu, wvs,
             wqm * D_MEM ** -0.5, wg]
    return jnp.concatenate(parts, axis=1).astype(BF16)


def _bias_row(sb_bias_l):
    b2 = sb_bias_l.astype(F32) * LOG2E
    hi = b2.astype(BF16).astype(F32)
    lo = (b2 - hi).astype(BF16).astype(F32)
    zq = jnp.zeros((H_SB, AUG), F32).at[:, D_SB].set(1.0).at[:, D_SB + 1].set(1.0)
    zk = jnp.zeros((H_SB, AUG), F32).at[:, D_SB].set(hi).at[:, D_SB + 1].set(lo)
    return jnp.concatenate([zq.reshape(1, -1), zk.reshape(1, -1)], axis=1)


def _spatial_operands(sp_w_l, sp_b_l, chunk, tm):
    reps = tm // chunk
    wt = jnp.tile(sp_w_l[:, :chunk, :chunk], (1, reps, reps))
    bt = jnp.tile(jnp.repeat(sp_b_l[:, :chunk].T, W_SP // SP_GROUPS, axis=1), (reps, 1))
    return wt, bt


def kernel(x_prompt, x_sample, mem_prompt, cache_sb_k, cache_sb_v, page_table, cache_mem_k,
           cache_mem_v, state_ffn_conv, norm_mix, w_in, sb_bias, sp_norm, sp_w, sp_b, mem_norm,
           w_mem_kv, w_branch, w_o, norm_ffn, w_up, conv_w, conv_b, w_down, norm_final):
    depth = w_in.shape[0]
    batch, seq, _ = x_prompt.shape
    n_seq, dec_seq, _ = x_sample.shape
    assert batch == 1 and dec_seq == SUBLANE and n_seq * dec_seq == ROW_BLOCK
    assert seq % (SB_Q_TILES * ROW_BLOCK) == 0 and page_table.shape[1] % PAGES_PER_STEP == 0

    hp = x_prompt.reshape(seq, D_MODEL)
    hs = x_sample.reshape(n_seq * dec_seq, D_MODEL)
    n_pool = cache_sb_k.shape[1]
    cache_k = jnp.transpose(cache_sb_k, (0, 1, 3, 4, 2)).reshape(depth, n_pool, W_SB, PAGE_SIZE)
    cache_v = jnp.transpose(cache_sb_v, (0, 1, 3, 4, 2)).reshape(depth, n_pool, W_SB, PAGE_SIZE)
    mem_kv = _memkv(mem_prompt.reshape(N_MEM, D_MODEL), mem_norm, w_mem_kv.astype(BF16))
    mem_k_s = cache_mem_k.reshape(depth * n_seq, N_MEM, W_MEM)
    mem_v_s = cache_mem_v.reshape(depth * n_seq, N_MEM, W_MEM)
    gfinal = norm_final.reshape(1, D_MODEL)
    head_eye = jnp.eye(H_SB, dtype=BF16)

    pk, pv, sk, sv, spv, pconv, sconv = [], [], [], [], [], [], []
    for l in range(depth):
        w_all = _fused_in_weight(w_in[l])
        brow = _bias_row(sb_bias[l])
        gain_mix = norm_mix[l].reshape(1, D_MODEL)
        spn = sp_norm[l].reshape(1, W_SP)
        wbr = w_branch[l].astype(BF16)
        wo = w_o[l].astype(BF16)
        gain_ffn = norm_ffn[l].reshape(1, D_MODEL)
        wup = w_up[l].astype(BF16)
        wdn = w_down[l].astype(BF16)
        cw = conv_w[l]
        cb = conv_b[l].reshape(1, 2 * D_FF)
        last = l == depth - 1

        qT, ka, k_p, v_p, vT, ug, vn, qm, gates = _inproj(hp, gain_mix, w_all, brow, spn, True)
        osb = _sb_prompt(qT, ka, vT)
        omem = _mem_attend(qm, mem_kv, mem_kv, lambda b, l=l: (l, 0, 0), lambda b, l=l: (l, 0, 1),
                           1, 2 * ROW_BLOCK)
        wt, bt = _spatial_operands(sp_w[l], sp_b[l], CHUNK, ROW_BLOCK)
        hp = _mix(hp, osb, omem, ug, vn, gates, wt, bt, wbr, wo, CHUNK)
        hp, up_last = _ffn(hp, gain_ffn, wup, cw, cb, wdn, gfinal, None, None, last)
        pk.append(k_p)
        pv.append(v_p)
        pconv.append(up_last[SUBLANE - (CONV_W - 1):, :])

        qT, ka, k_s, v_s, vT, ug, vn, qm, gates = _inproj(hs, gain_mix, w_all, brow, spn, False)
        q = qT.reshape(H_SB, AUG, n_seq, dec_seq)[:, :D_SB]
        qbd = jnp.einsum('hdbt,hg->bhtgd', q, head_eye).reshape(n_seq, H_SB * dec_seq, W_SB)
        bias_full = jnp.broadcast_to(
            jnp.repeat(sb_bias[l].astype(F32) * LOG2E, dec_seq)[:, None], (H_SB * dec_seq, PAGE_SIZE))
        osb = _sb_sample(l, page_table, qbd, bias_full, k_s, v_s, cache_k, cache_v)
        at = lambda b, l=l: (l * n_seq + b, 0, 0)
        omem = _mem_attend(qm, mem_k_s, mem_v_s, at, at, n_seq, dec_seq)
        wt, bt = _spatial_operands(sp_w[l], sp_b[l], dec_seq, ROW_BLOCK)
        hs = _mix(hs, osb, omem, ug, vn, gates, wt, bt, wbr, wo, dec_seq)
        st = state_ffn_conv[l]
        s1 = jnp.repeat(st[:, 1:2, :], dec_seq, axis=1).reshape(n_seq * dec_seq, 2 * D_FF)
        s2 = jnp.tile(st, (1, dec_seq // (CONV_W - 1), 1)).reshape(n_seq * dec_seq, 2 * D_FF)
        hs, up_s = _ffn(hs, gain_ffn, wup, cw, cb, wdn, gfinal, (s1, s2), dec_seq, last)
        sk.append(k_s)
        sv.append(v_s)
        spv.append(vn)
        sconv.append(up_s.reshape(n_seq, dec_seq, 2 * D_FF)[:, dec_seq - (CONV_W - 1):, :])

    n_pp = seq // PAGE_SIZE
    y_prompt = hp.reshape(batch, seq, D_MODEL)
    y_sample = hs.reshape(n_seq, dec_seq, D_MODEL)
    return (
        y_prompt,
        y_sample,
        jnp.transpose(jnp.stack(pk).reshape(depth, n_pp, H_SB, D_SB, PAGE_SIZE), (0, 1, 4, 2, 3)),
        jnp.transpose(jnp.stack(pv).reshape(depth, n_pp, H_SB, D_SB, PAGE_SIZE), (0, 1, 4, 2, 3)),
        jnp.stack(sk).reshape(depth, n_seq, dec_seq, H_SB, D_SB),
        jnp.stack(sv).reshape(depth, n_seq, dec_seq, H_SB, D_SB),
        mem_kv[:, :, :W_MEM].reshape(depth, batch, N_MEM, H_MEM, D_MEM),
        mem_kv[:, :, W_MEM:].reshape(depth, batch, N_MEM, H_MEM, D_MEM),
        jnp.stack(spv).reshape(depth, n_seq, dec_seq, W_SP),
        jnp.stack(pconv).reshape(depth, batch, CONV_W - 1, 2 * D_FF),
        jnp.stack(sconv).reshape(depth, n_seq, CONV_W - 1, 2 * D_FF),
    )
```

```python
import functools
import math

import jax
import jax.numpy as jnp
from jax import lax
from jax.experimental import pallas as pl
from jax.experimental.pallas import tpu as pltpu

F32 = jnp.float32
BF16 = jnp.bfloat16

D_MODEL = 1024
H_SB = 8
D_SB = 64
W_SB = H_SB * D_SB
SP_GROUPS = 8
W_SP = 512
CHUNK = 128
N_MEM = 256
H_MEM = 4
D_MEM = 128
W_MEM = H_MEM * D_MEM
N_BRANCH = 3
D_FF = 2816
CONV_W = 3
EPS = 1e-6
PAGE_SIZE = 128

LOG2E = 1.4426950408889634
MAX_LOGIT2 = 126.0
LANE = 128
SUBLANE = 8
ROW_BLOCK = 256
AUG = 2 * D_SB
FFN_COLS = 256
SB_HEADS = 4
SB_Q_TILES = 4
SB_KEY_TILES = 4
SB_STAGE_LAG = 2
SBS_SLOTS = 3
PAGES_PER_STEP = 8
VMEM_LIMIT = 56 * 1024 * 1024

C_QA = 0
C_KA = C_QA + H_SB * AUG
C_K = C_KA + H_SB * AUG
C_V = C_K + W_SB
C_U = C_V + W_SB
C_VS = C_U + W_SP
C_QM = C_VS + W_SP
C_G = C_QM + W_MEM
C_END = C_G + N_BRANCH * D_MODEL


def _params(n_axes, limit=VMEM_LIMIT):
    return pltpu.CompilerParams(
        dimension_semantics=("arbitrary",) * n_axes, vmem_limit_bytes=limit)


def _resident(shape, index_map):
    return pl.BlockSpec(shape, index_map, pipeline_mode=pl.Buffered(1))


def _rms(x, gain):
    return x * lax.rsqrt(jnp.mean(x * x, axis=-1, keepdims=True) + EPS) * gain


def _gelu(x):
    c = math.sqrt(2.0 / math.pi)
    return x * (0.5 * (1.0 + jnp.tanh(c * (x + 0.044715 * (x * x * x)))))


def _dot(a, b):
    return jnp.dot(a, b, preferred_element_type=F32)


def _softplus2(z):
    return jnp.log2(1.0 + jnp.exp2(z))


def _memkv_kernel(mem_ref, g_ref, w_ref, kv_ref):
    a = _rms(mem_ref[...], g_ref[...]).astype(BF16)
    kv_ref[...] = _dot(a, w_ref[...])


def _memkv(mem, mem_norm, w_mem_kv_bf):
    depth = w_mem_kv_bf.shape[0]
    return pl.pallas_call(
        _memkv_kernel,
        grid=(depth,),
        in_specs=[
            pl.BlockSpec((N_MEM, D_MODEL), lambda l: (0, 0)),
            pl.BlockSpec((None, 1, D_MODEL), lambda l: (l, 0, 0)),
            pl.BlockSpec((None, D_MODEL, 2 * W_MEM), lambda l: (l, 0, 0)),
        ],
        out_specs=pl.BlockSpec((None, N_MEM, 2 * W_MEM), lambda l: (l, 0, 0)),
        out_shape=jax.ShapeDtypeStruct((depth, N_MEM, 2 * W_MEM), F32),
        compiler_params=_params(1),
        name="memkv",
    )(mem, mem_norm.reshape(depth, 1, D_MODEL), w_mem_kv_bf)


def _inproj_kernel(x_ref, g_ref, w_ref, brow_ref, spn_ref,
                   qT_ref, ka_ref, k_ref, v_ref, vT_ref, ug_ref, vn_ref, qm_ref, gate_ref, *, paged):
    a = _rms(x_ref[...], g_ref[...]).astype(BF16)

    def mm(c0, c1):
        return _dot(a, w_ref[:, c0:c1])

    qa = mm(C_QA, C_KA) + brow_ref[:, :H_SB * AUG]
    qT_ref[...] = qa.T.astype(BF16)
    ka = mm(C_KA, C_K) + brow_ref[:, H_SB * AUG:]
    ka_ref[...] = ka.astype(BF16)
    k = mm(C_K, C_V)
    v = mm(C_V, C_U)
    vT = v.T
    vT_ref[...] = vT.astype(BF16)
    if paged:
        kT = k.T
        for j in range(k_ref.shape[0]):
            k_ref[j] = kT[:, j * PAGE_SIZE:(j + 1) * PAGE_SIZE]
            v_ref[j] = vT[:, j * PAGE_SIZE:(j + 1) * PAGE_SIZE]
    else:
        k_ref[...] = k
        v_ref[...] = v
    ug_ref[...] = _gelu(mm(C_U, C_VS)).astype(BF16)
    vn_ref[...] = _rms(_gelu(mm(C_VS, C_QM)), spn_ref[...])
    qm_ref[...] = mm(C_QM, C_G)
    for i in range(N_BRANCH):
        g = mm(C_G + i * D_MODEL, C_G + (i + 1) * D_MODEL)
        gate_ref[:, i * D_MODEL:(i + 1) * D_MODEL] = jax.nn.sigmoid(g)


def _inproj(x, gain, w_all, brow, sp_norm, paged):
    rows = x.shape[0]
    tm = ROW_BLOCK
    nb = rows // tm
    row = lambda i: (i, 0)
    const = lambda i: (0, 0)
    if paged:
        ppb = tm // PAGE_SIZE
        kv_shape = jax.ShapeDtypeStruct((rows // PAGE_SIZE, W_SB, PAGE_SIZE), F32)
        kv_spec = pl.BlockSpec((ppb, W_SB, PAGE_SIZE), lambda i: (i, 0, 0))
    else:
        kv_shape = jax.ShapeDtypeStruct((rows, W_SB), F32)
        kv_spec = pl.BlockSpec((tm, W_SB), row)
    out_shapes = (
        jax.ShapeDtypeStruct((nb, H_SB * AUG, tm), BF16),
        jax.ShapeDtypeStruct((rows, H_SB * AUG), BF16),
        kv_shape,
        kv_shape,
        jax.ShapeDtypeStruct((nb, W_SB, tm), BF16),
        jax.ShapeDtypeStruct((rows, W_SP), BF16),
        jax.ShapeDtypeStruct((rows, W_SP), F32),
        jax.ShapeDtypeStruct((rows, W_MEM), F32),
        jax.ShapeDtypeStruct((rows, N_BRANCH * D_MODEL), F32),
    )
    out_specs = (
        pl.BlockSpec((None, H_SB * AUG, tm), lambda i: (i, 0, 0)),
        pl.BlockSpec((tm, H_SB * AUG), row),
        kv_spec,
        kv_spec,
        pl.BlockSpec((None, W_SB, tm), lambda i: (i, 0, 0)),
        pl.BlockSpec((tm, W_SP), row),
        pl.BlockSpec((tm, W_SP), row),
        pl.BlockSpec((tm, W_MEM), row),
        pl.BlockSpec((tm, N_BRANCH * D_MODEL), row),
    )
    return pl.pallas_call(
        functools.partial(_inproj_kernel, paged=paged),
        grid=(nb,),
        in_specs=[
            pl.BlockSpec((tm, D_MODEL), row),
            _resident((1, D_MODEL), const),
            _resident((D_MODEL, C_END), const),
            _resident((1, 2 * H_SB * AUG), const),
            _resident((1, W_SP), const),
        ],
        out_specs=out_specs,
        out_shape=out_shapes,
        compiler_params=_params(1),
        name="inproj",
    )(x, gain, w_all, brow, sp_norm)


def _sb_chains(chains, upper):
    depth = max(len(tiles) for _, tiles in chains)
    order = [(c, p) for p in range(depth) for c, (_, tiles) in enumerate(chains) if p < len(tiles)]
    carries = [carry for carry, _ in chains]
    totals = [None] * len(chains)
    zs, mids = {}, {}
    for step in range(len(order) + 2 * SB_STAGE_LAG):
        if step < len(order):
            c, p = order[step]
            k, q, _, _ = chains[c][1][p]
            zs[c, p] = jnp.minimum(_dot(k, q), MAX_LOGIT2)
        if 0 <= step - SB_STAGE_LAG < len(order):
            c, p = order[step - SB_STAGE_LAG]
            valid = chains[c][1][p][3]
            z = zs.pop((c, p))
            sp = _softplus2(z)
            if valid is not None:
                sp = jnp.where(valid, sp, 0.0)
            mids[c, p] = (z, _dot(upper, sp.astype(BF16)))
        if 0 <= step - 2 * SB_STAGE_LAG < len(order):
            c, p = order[step - 2 * SB_STAGE_LAG]
            _, _, vT, valid = chains[c][1][p]
            z, newer = mids.pop((c, p))
            a = jnp.exp2((z - newer - carries[c]).astype(BF16))
            if valid is not None:
                a = jnp.where(valid, a, jnp.zeros_like(a))
            o = _dot(vT, a)
            totals[c] = o if totals[c] is None else totals[c] + o
            carries[c] = carries[c] + newer[0:1, :]
    return list(zip(totals, carries))


def _sbp_kernel(qT_ref, ka_ref, vT_ref, o_ref, acc_ref):
    t = ROW_BLOCK
    qi = pl.program_id(1)
    rk = lax.broadcasted_iota(jnp.int32, (t, t), 0)
    cq = lax.broadcasted_iota(jnp.int32, (t, t), 1)
    upper = jnp.where(cq >= rk, 1.0, 0.0).astype(BF16)
    causal = rk < cq
    zero = jnp.zeros((1, t), F32)
    streams = [(hh, part) for hh in range(SB_HEADS) for part in range(SB_Q_TILES)]

    def tile(kb, hh, part, valid):
        r0 = pl.multiple_of(kb * t, t)
        return (ka_ref[pl.ds(r0, t), hh * AUG:(hh + 1) * AUG],
                qT_ref[part, hh * AUG:(hh + 1) * AUG, :],
                vT_ref[kb, hh * D_SB:(hh + 1) * D_SB, :], valid)

    def acc_at(hh, part):
        return (slice(hh * D_SB, (hh + 1) * D_SB), slice(part * t, (part + 1) * t))

    lo = SB_Q_TILES * qi
    chains = [(zero, [tile(lo + part, hh, part, causal)]
               + [tile(lo + part - 1 - d, hh, part, None) for d in range(part)])
              for hh, part in streams]
    carries = []
    for (hh, part), (o, carry) in zip(streams, _sb_chains(chains, upper)):
        acc_ref[acc_at(hh, part)] = o
        carries.append(carry)

    def body(jj, carries):
        kb0 = lo - 1 - jj * SB_KEY_TILES
        chains = [(carries[i], [tile(kb0 - d, hh, part, None) for d in range(SB_KEY_TILES)])
                  for i, (hh, part) in enumerate(streams)]
        out = []
        for (hh, part), (o, carry) in zip(streams, _sb_chains(chains, upper)):
            acc_ref[acc_at(hh, part)] += o
            out.append(carry)
        return tuple(out)

    lax.fori_loop(0, lo // SB_KEY_TILES, body, tuple(carries))
    o_ref[...] = acc_ref[...].T


def _sb_prompt(qT, ka, vT):
    nb = qT.shape[0]
    t = ROW_BLOCK
    rows = nb * t
    return pl.pallas_call(
        _sbp_kernel,
        grid=(H_SB // SB_HEADS, nb // SB_Q_TILES),
        in_specs=[
            pl.BlockSpec((SB_Q_TILES, SB_HEADS * AUG, t), lambda hg, qi: (qi, hg, 0)),
            _resident((rows, SB_HEADS * AUG), lambda hg, qi: (0, hg)),
            _resident((nb, SB_HEADS * D_SB, t), lambda hg, qi: (0, hg, 0)),
        ],
        out_specs=pl.BlockSpec((SB_Q_TILES * t, SB_HEADS * D_SB), lambda hg, qi: (qi, hg)),
        out_shape=jax.ShapeDtypeStruct((rows, W_SB), F32),
        scratch_shapes=[pltpu.VMEM((SB_HEADS * D_SB, SB_Q_TILES * t), F32)],
        compiler_params=_params(2),
        name="sb_prompt",
    )(qT, ka, vT)


def _sbs_kernel(pt_ref, qbd_ref, bias_ref, kn_ref, vn_ref, ck_hbm, cv_hbm, o_ref,
                kbuf, vbuf, sems, *, layer, n_seq, n_pages):
    g_pages = PAGES_PER_STEP
    n_groups = n_pages // g_pages
    b = pl.program_id(0)
    nq = H_SB * SUBLANE
    qbd = qbd_ref[...]
    bias = bias_ref[...]
    rj = lax.broadcasted_iota(jnp.int32, (PAGE_SIZE, 2 * PAGE_SIZE), 0)
    cs = lax.broadcasted_iota(jnp.int32, (PAGE_SIZE, 2 * PAGE_SIZE), 1)
    newer_tot = jnp.where((rj > cs) | (cs >= PAGE_SIZE), 1.0, 0.0).astype(BF16)
    nt_dims = (((1,), (1,)), ((), ()))

    def group_copies(gidx):
        seq = gidx // n_groups
        group = gidx % n_groups
        slot = gidx % SBS_SLOTS
        copies = []
        for i in range(g_pages):
            page = pt_ref[seq, n_pages - 1 - (group * g_pages + i)]
            lanes = pl.ds(i * PAGE_SIZE, PAGE_SIZE)
            copies.append(pltpu.make_async_copy(
                ck_hbm.at[layer, page], kbuf.at[slot, :, lanes], sems.at[0, slot, i]))
            copies.append(pltpu.make_async_copy(
                cv_hbm.at[layer, page], vbuf.at[slot, :, lanes], sems.at[1, slot, i]))
        return copies

    def start_group(gidx):
        for c in group_copies(gidx):
            c.start()

    def wait_group(gidx):
        for c in group_copies(gidx):
            c.wait()

    def attend(z, carry, valid, av):
        z = jnp.minimum(z, MAX_LOGIT2)
        sp = _softplus2(z)
        log_beta = z - sp
        if valid is not None:
            sp = jnp.where(valid, sp, 0.0)
        spb = sp.astype(BF16)
        n = z.shape[1] // PAGE_SIZE
        nts = [_dot(spb[:, i * PAGE_SIZE:(i + 1) * PAGE_SIZE], newer_tot) for i in range(n)]
        parts = []
        for i, nt in enumerate(nts):
            lanes = slice(i * PAGE_SIZE, (i + 1) * PAGE_SIZE)
            parts.append(jnp.exp2(log_beta[:, lanes] - nt[:, :PAGE_SIZE] - carry))
            carry = carry + nt[:, PAGE_SIZE:]
        a = parts[0] if n == 1 else jnp.concatenate(parts, axis=1)
        if valid is not None:
            a = jnp.where(valid, a, 0.0)
        return av(a.astype(BF16)), carry

    @pl.when(b == 0)
    def _():
        for gidx in range(SBS_SLOTS - 1):
            start_group(gidx)

    pad = jnp.zeros((PAGE_SIZE - SUBLANE, W_SB), F32)
    kn = jnp.concatenate([kn_ref[...], pad], axis=0).astype(BF16)
    vn = jnp.concatenate([vn_ref[...], pad], axis=0).astype(BF16)
    tq = lax.broadcasted_iota(jnp.int32, (nq, PAGE_SIZE), 0) % SUBLANE
    kj = lax.broadcasted_iota(jnp.int32, (nq, PAGE_SIZE), 1)
    z = lax.dot_general(qbd, kn, nt_dims, preferred_element_type=F32) + bias
    acc, carry = attend(z, jnp.zeros((nq, PAGE_SIZE), F32), kj < tq, lambda a: _dot(a, vn))

    bias_wide = jnp.concatenate([bias] * g_pages, axis=1)

    def body(group, state):
        acc, carry = state
        gidx = b * n_groups + group
        slot = gidx % SBS_SLOTS

        @pl.when(gidx + SBS_SLOTS - 1 < n_seq * n_groups)
        def _():
            start_group(gidx + SBS_SLOTS - 1)

        wait_group(gidx)
        kb = kbuf[slot].astype(BF16)
        vb = vbuf[slot].astype(BF16)
        z = _dot(qbd, kb) + bias_wide
        o, carry = attend(z, carry, None,
                          lambda a: lax.dot_general(a, vb, nt_dims, preferred_element_type=F32))
        return acc + o, carry

    acc, _ = lax.fori_loop(0, n_groups, body, (acc, carry))

    rh = lax.broadcasted_iota(jnp.int32, acc.shape, 0) // SUBLANE
    ch = lax.broadcasted_iota(jnp.int32, acc.shape, 1) // D_SB
    own = jnp.where(rh == ch, acc, 0.0)
    out = own[0:SUBLANE, :]
    for h in range(1, H_SB):
        out = out + own[h * SUBLANE:(h + 1) * SUBLANE, :]
    o_ref[...] = out


def _sb_sample(layer, page_table, qbd, bias_full, k_new, v_new, cache_k, cache_v):
    n_seq, n_pages = page_table.shape
    g_pages = PAGES_PER_STEP
    assert n_pages % g_pages == 0 and n_seq * (n_pages // g_pages) >= SBS_SLOTS - 1
    nq = H_SB * SUBLANE
    grid_spec = pltpu.PrefetchScalarGridSpec(
        num_scalar_prefetch=1,
        grid=(n_seq,),
        in_specs=[
            pl.BlockSpec((None, nq, W_SB), lambda b, pt: (b, 0, 0)),
            pl.BlockSpec((nq, PAGE_SIZE), lambda b, pt: (0, 0)),
            pl.BlockSpec((SUBLANE, W_SB), lambda b, pt: (b, 0)),
            pl.BlockSpec((SUBLANE, W_SB), lambda b, pt: (b, 0)),
            pl.BlockSpec(memory_space=pl.ANY),
            pl.BlockSpec(memory_space=pl.ANY),
        ],
        out_specs=pl.BlockSpec((SUBLANE, W_SB), lambda b, pt: (b, 0)),
        scratch_shapes=[
            pltpu.VMEM((SBS_SLOTS, W_SB, g_pages * PAGE_SIZE), F32),
            pltpu.VMEM((SBS_SLOTS, W_SB, g_pages * PAGE_SIZE), F32),
            pltpu.SemaphoreType.DMA((2, SBS_SLOTS, g_pages)),
        ],
    )
    return pl.pallas_call(
        functools.partial(_sbs_kernel, layer=layer, n_seq=n_seq, n_pages=n_pages),
        grid_spec=grid_spec,
        out_shape=jax.ShapeDtypeStruct((n_seq * SUBLANE, W_SB), F32),
        compiler_params=_params(1),
        name="sb_sample",
    )(page_table, qbd, bias_full, k_new, v_new, cache_k, cache_v)


def _memattn_kernel(q_ref, mk_ref, mv_ref, o_ref):
    for h in range(H_MEM):
        cols = slice(h * D_MEM, (h + 1) * D_MEM)
        q = q_ref[:, cols].astype(BF16)
        s = lax.dot_general(q, mk_ref[:, cols].astype(BF16), (((1,), (1,)), ((), ())),
                            preferred_element_type=F32)
        e = jnp.exp(s - jnp.max(s, axis=-1, keepdims=True))
        p = e / jnp.sum(e, axis=-1, keepdims=True)
        o_ref[:, cols] = _dot(p.astype(BF16), mv_ref[:, cols].astype(BF16))


def _mem_attend(q, mk, mv, mk_at, mv_at, n_sets, tq):
    nrb = q.shape[0] // (n_sets * tq)
    return pl.pallas_call(
        _memattn_kernel,
        grid=(n_sets, nrb),
        in_specs=[
            pl.BlockSpec((tq, W_MEM), lambda b, r: (b * nrb + r, 0)),
            pl.BlockSpec((None, N_MEM, W_MEM), lambda b, r: mk_at(b)),
            pl.BlockSpec((None, N_MEM, W_MEM), lambda b, r: mv_at(b)),
        ],
        out_specs=pl.BlockSpec((tq, W_MEM), lambda b, r: (b * nrb + r, 0)),
        out_shape=jax.ShapeDtypeStruct(q.shape, F32),
        compiler_params=_params(2),
        name="mem_attend",
    )(q, mk, mv)


def _mix_kernel(h_ref, osb_ref, omem_ref, ug_ref, vn_ref, gate_ref, wt_ref, bt_ref,
                wbr_ref, wo_ref, out_ref, *, chunk):
    tm = h_ref.shape[0]
    shift = chunk.bit_length() - 1
    r = lax.broadcasted_iota(jnp.int32, (tm, tm), 0)
    c = lax.broadcasted_iota(jnp.int32, (tm, tm), 1)
    causal = (c <= r) & ((r >> shift) == (c >> shift))
    lane = lax.broadcasted_iota(jnp.int32, (tm, LANE), 1)
    gw = W_SP // SP_GROUPS
    vnb = vn_ref[...].astype(BF16)
    tiles = []
    for j in range(W_SP // LANE):
        vt = vnb[:, j * LANE:(j + 1) * LANE]
        per_group = []
        for g in range(LANE // gw):
            wg = wt_ref[j * (LANE // gw) + g]
            w = jnp.where(causal, wg, jnp.zeros_like(wg))
            per_group.append(_dot(w, vt))
        tiles.append(jnp.where(lane < gw, per_group[0], per_group[1]))
    s = jnp.concatenate(tiles, axis=1) + bt_ref[...]
    osp = ug_ref[...].astype(F32) * s

    mixed = None
    for i, o in enumerate((osb_ref[...], osp, omem_ref[...])):
        br = _dot(o.astype(BF16), wbr_ref[i]) * gate_ref[:, i * D_MODEL:(i + 1) * D_MODEL]
        mixed = br if mixed is None else mixed + br
    out_ref[...] = h_ref[...] + _dot(mixed.astype(BF16), wo_ref[...])


def _mix(h, osb, omem, ug, vn, gates, wt, bt, wbr, wo, chunk):
    rows = h.shape[0]
    tm = ROW_BLOCK
    row = lambda i: (i, 0)
    return pl.pallas_call(
        functools.partial(_mix_kernel, chunk=chunk),
        grid=(rows // tm,),
        in_specs=[
            pl.BlockSpec((tm, D_MODEL), row),
            pl.BlockSpec((tm, W_SB), row),
            pl.BlockSpec((tm, W_MEM), row),
            pl.BlockSpec((tm, W_SP), row),
            pl.BlockSpec((tm, W_SP), row),
            pl.BlockSpec((tm, N_BRANCH * D_MODEL), row),
            _resident((SP_GROUPS, tm, tm), lambda i: (0, 0, 0)),
            _resident((tm, W_SP), lambda i: (0, 0)),
            _resident((N_BRANCH, W_SB, D_MODEL), lambda i: (0, 0, 0)),
            _resident((D_MODEL, D_MODEL), lambda i: (0, 0)),
        ],
        out_specs=pl.BlockSpec((tm, D_MODEL), row),
        out_shape=jax.ShapeDtypeStruct((rows, D_MODEL), F32),
        compiler_params=_params(1),
        name="mix",
    )(h, osb, omem, ug, vn, gates, wt, bt, wbr, wo)


def _ffn_kernel(*refs, seq_len, final_norm):
    h_ref, g_ref, wup_ref, cw_ref, cb_ref, wdn_ref, gf_ref = refs[:7]
    if seq_len is None:
        out_ref, up_ref, act_ref, carry_ref = refs[7:]
    else:
        s1_ref, s2_ref = refs[7:9]
        out_ref, up_ref, act_ref = refs[9:]
    tm = h_ref.shape[0]
    i = pl.program_id(0)
    h = h_ref[...]
    f = _rms(h, g_ref[...]).astype(BF16)
    pos = lax.broadcasted_iota(jnp.int32, (tm, FFN_COLS), 0)
    if seq_len is None:
        @pl.when(i == 0)
        def _():
            carry_ref[...] = jnp.zeros_like(carry_ref)
    else:
        pos = pos % seq_len

    for c in range(D_FF // FFN_COLS):
        conv = []
        for part in range(2):
            cols = slice(part * D_FF + c * FFN_COLS, part * D_FF + (c + 1) * FFN_COLS)
            up = _dot(f, wup_ref[:, cols])
            if seq_len is None:
                prev = carry_ref[:, cols]
                p1 = jnp.broadcast_to(prev[SUBLANE - 1:SUBLANE, :], up.shape)
                p2 = jnp.where(pos == 0, prev[SUBLANE - 2:SUBLANE - 1, :], p1)
                carry_ref[:, cols] = up[tm - SUBLANE:, :]
                up_ref[:, cols] = up[tm - SUBLANE:, :]
            else:
                p1 = s1_ref[:, cols]
                p2 = s2_ref[:, cols]
                up_ref[:, cols] = up
            x1 = jnp.where(pos >= 1, pltpu.roll(up, 1, 0), p1)
            x2 = jnp.where(pos >= 2, pltpu.roll(up, 2, 0), p2)
            conv.append(cb_ref[:, cols] + cw_ref[0:1, cols] * x2 + cw_ref[1:2, cols] * x1
                        + cw_ref[2:3, cols] * up)
        act_ref[:, c * FFN_COLS:(c + 1) * FFN_COLS] = (_gelu(conv[0]) * conv[1]).astype(BF16)

    y = h + _dot(act_ref[...], wdn_ref[...])
    if final_norm:
        y = _rms(y, gf_ref[...])
    out_ref[...] = y


def _ffn(h, gain, wup, cw, cb, wdn, gfinal, prev_rows, seq_len, final_norm):
    rows = h.shape[0]
    tm = ROW_BLOCK
    row = lambda i: (i, 0)
    const = lambda i: (0, 0)
    in_specs = [
        pl.BlockSpec((tm, D_MODEL), row),
        _resident((1, D_MODEL), const),
        _resident((D_MODEL, 2 * D_FF), const),
        _resident((CONV_W, 2 * D_FF), const),
        _resident((1, 2 * D_FF), const),
        _resident((D_FF, D_MODEL), const),
        _resident((1, D_MODEL), const),
    ]
    args = [h, gain, wup, cw, cb, wdn, gfinal]
    scratch = [pltpu.VMEM((tm, D_FF), BF16)]
    if seq_len is None:
        up_rows = SUBLANE
        up_spec = pl.BlockSpec((SUBLANE, 2 * D_FF), const)
        scratch.append(pltpu.VMEM((SUBLANE, 2 * D_FF), F32))
    else:
        up_rows = rows
        up_spec = pl.BlockSpec((tm, 2 * D_FF), row)
        in_specs += [_resident((tm, 2 * D_FF), row)] * 2
        args += list(prev_rows)
    return pl.pallas_call(
        functools.partial(_ffn_kernel, seq_len=seq_len, final_norm=final_norm),
        grid=(rows // tm,),
        in_specs=in_specs,
        out_specs=(pl.BlockSpec((tm, D_MODEL), row), up_spec),
        out_shape=(jax.ShapeDtypeStruct((rows, D_MODEL), F32),
                   jax.ShapeDtypeStruct((up_rows, 2 * D_FF), F32)),
        scratch_shapes=scratch,
        compiler_params=_params(1),
        name="ffn",
    )(*args)


def _augment(w, scale):
    w = (w * scale).reshape(D_MODEL, H_SB, D_SB)
    return jnp.pad(w, ((0, 0), (0, 0), (0, AUG - D_SB))).reshape(D_MODEL, H_SB * AUG)


def _fused_in_weight(w_in_l):
    sizes = [W_SB, W_SB, W_SB, W_SP, W_SP, W_MEM, N_BRANCH * D_MODEL]
    offs = [0]
    for s in sizes:
        offs.append(offs[-1] + s)
    wq, wk, wv, wu, wvs, wqm, wg = [w_in_l[:, offs[i]:offs[i + 1]] for i in range(7)]
    parts = [_augment(wq, LOG2E * D_SB ** -0.5), _augment(wk, 1.0), wk, wv, wu, wvs,
             wqm * D_MEM ** -0.5, wg]
    return jnp.concatenate(parts, axis=1).astype(BF16)


def _bias_row(sb_bias_l):
    b2 = sb_bias_l.astype(F32) * LOG2E
    hi = b2.astype(BF16).astype(F32)
    lo = (b2 - hi).astype(BF16).astype(F32)
    zq = jnp.zeros((H_SB, AUG), F32).at[:, D_SB].set(1.0).at[:, D_SB + 1].set(1.0)
    zk = jnp.zeros((H_SB, AUG), F32).at[:, D_SB].set(hi).at[:, D_SB + 1].set(lo)
    return jnp.concatenate([zq.reshape(1, -1), zk.reshape(1, -1)], axis=1)


def _spatial_operands(sp_w_l, sp_b_l, chunk, tm):
    reps = tm // chunk
    wt = jnp.tile(sp_w_l[:, :chunk, :chunk].astype(BF16), (1, reps, reps))
    bt = jnp.tile(jnp.repeat(sp_b_l[:, :chunk].T, W_SP // SP_GROUPS, axis=1), (reps, 1))
    return wt, bt


def kernel(x_prompt, x_sample, mem_prompt, cache_sb_k, cache_sb_v, page_table, cache_mem_k,
           cache_mem_v, state_ffn_conv, norm_mix, w_in, sb_bias, sp_norm, sp_w, sp_b, mem_norm,
           w_mem_kv, w_branch, w_o, norm_ffn, w_up, conv_w, conv_b, w_down, norm_final):
    depth = w_in.shape[0]
    batch, seq, _ = x_prompt.shape
    n_seq, dec_seq, _ = x_sample.shape
    assert batch == 1 and dec_seq == SUBLANE and n_seq * dec_seq == ROW_BLOCK
    assert seq % (SB_Q_TILES * ROW_BLOCK) == 0 and page_table.shape[1] % PAGES_PER_STEP == 0

    hp = x_prompt.reshape(seq, D_MODEL)
    hs = x_sample.reshape(n_seq * dec_seq, D_MODEL)
    n_pool = cache_sb_k.shape[1]
    cache_k = jnp.transpose(cache_sb_k, (0, 1, 3, 4, 2)).reshape(depth, n_pool, W_SB, PAGE_SIZE)
    cache_v = jnp.transpose(cache_sb_v, (0, 1, 3, 4, 2)).reshape(depth, n_pool, W_SB, PAGE_SIZE)
    mem_kv = _memkv(mem_prompt.reshape(N_MEM, D_MODEL), mem_norm, w_mem_kv.astype(BF16))
    mem_k_s = cache_mem_k.reshape(depth * n_seq, N_MEM, W_MEM)
    mem_v_s = cache_mem_v.reshape(depth * n_seq, N_MEM, W_MEM)
    gfinal = norm_final.reshape(1, D_MODEL)
    head_eye = jnp.eye(H_SB, dtype=BF16)

    pk, pv, sk, sv, spv, pconv, sconv = [], [], [], [], [], [], []
    for l in range(depth):
        w_all = _fused_in_weight(w_in[l])
        brow = _bias_row(sb_bias[l])
        gain_mix = norm_mix[l].reshape(1, D_MODEL)
        spn = sp_norm[l].reshape(1, W_SP)
        wbr = w_branch[l].astype(BF16)
        wo = w_o[l].astype(BF16)
        gain_ffn = norm_ffn[l].reshape(1, D_MODEL)
        wup = w_up[l].astype(BF16)
        wdn = w_down[l].astype(BF16)
        cw = conv_w[l]
        cb = conv_b[l].reshape(1, 2 * D_FF)
        last = l == depth - 1

        qT, ka, k_p, v_p, vT, ug, vn, qm, gates = _inproj(hp, gain_mix, w_all, brow, spn, True)
        osb = _sb_prompt(qT, ka, vT)
        omem = _mem_attend(qm, mem_kv, mem_kv, lambda b, l=l: (l, 0, 0), lambda b, l=l: (l, 0, 1),
                           1, 2 * ROW_BLOCK)
        wt, bt = _spatial_operands(sp_w[l], sp_b[l], CHUNK, ROW_BLOCK)
        hp = _mix(hp, osb, omem, ug, vn, gates, wt, bt, wbr, wo, CHUNK)
        hp, up_last = _ffn(hp, gain_ffn, wup, cw, cb, wdn, gfinal, None, None, last)
        pk.append(k_p)
        pv.append(v_p)
        pconv.append(up_last[SUBLANE - (CONV_W - 1):, :])

        qT, ka, k_s, v_s, vT, ug, vn, qm, gates = _inproj(hs, gain_mix, w_all, brow, spn, False)
        q = qT.reshape(H_SB, AUG, n_seq, dec_seq)[:, :D_SB]
        qbd = jnp.einsum('hdbt,hg->bhtgd', q, head_eye).reshape(n_seq, H_SB * dec_seq, W_SB)
        bias_full = jnp.broadcast_to(
            jnp.repeat(sb_bias[l].astype(F32) * LOG2E, dec_seq)[:, None], (H_SB * dec_seq, PAGE_SIZE))
        osb = _sb_sample(l, page_table, qbd, bias_full, k_s, v_s, cache_k, cache_v)
        at = lambda b, l=l: (l * n_seq + b, 0, 0)
        omem = _mem_attend(qm, mem_k_s, mem_v_s, at, at, n_seq, dec_seq)
        wt, bt = _spatial_operands(sp_w[l], sp_b[l], dec_seq, ROW_BLOCK)
        hs = _mix(hs, osb, omem, ug, vn, gates, wt, bt, wbr, wo, dec_seq)
        st = state_ffn_conv[l]
        s1 = jnp.repeat(st[:, 1:2, :], dec_seq, axis=1).reshape(n_seq * dec_seq, 2 * D_FF)
        s2 = jnp.tile(st, (1, dec_seq // (CONV_W - 1), 1)).reshape(n_seq * dec_seq, 2 * D_FF)
        hs, up_s = _ffn(hs, gain_ffn, wup, cw, cb, wdn, gfinal, (s1, s2), dec_seq, last)
        sk.append(k_s)
        sv.append(v_s)
        spv.append(vn)
        sconv.append(up_s.reshape(n_seq, dec_seq, 2 * D_FF)[:, dec_seq - (CONV_W - 1):, :])

    n_pp = seq // PAGE_SIZE
    y_prompt = hp.reshape(batch, seq, D_MODEL)
    y_sample = hs.reshape(n_seq, dec_seq, D_MODEL)
    return (
        y_prompt,
        y_sample,
        jnp.transpose(jnp.stack(pk).reshape(depth, n_pp, H_SB, D_SB, PAGE_SIZE), (0, 1, 4, 2, 3)),
        jnp.transpose(jnp.stack(pv).reshape(depth, n_pp, H_SB, D_SB, PAGE_SIZE), (0, 1, 4, 2, 3)),
        jnp.stack(sk).reshape(depth, n_seq, dec_seq, H_SB, D_SB),
        jnp.stack(sv).reshape(depth, n_seq, dec_seq, H_SB, D_SB),
        mem_kv[:, :, :W_MEM].reshape(depth, batch, N_MEM, H_MEM, D_MEM),
        mem_kv[:, :, W_MEM:].reshape(depth, batch, N_MEM, H_MEM, D_MEM),
        jnp.stack(spv).reshape(depth, n_seq, dec_seq, W_SP),
        jnp.stack(pconv).reshape(depth, batch, CONV_W - 1, 2 * D_FF),
        jnp.stack(sconv).reshape(depth, n_seq, CONV_W - 1, 2 * D_FF),
    )
```

```python
import functools
import math

import jax
import jax.numpy as jnp
from jax import lax
from jax.experimental import pallas as pl
from jax.experimental.pallas import tpu as pltpu

F32 = jnp.float32
BF16 = jnp.bfloat16

D_MODEL = 1024
H_SB = 8
D_SB = 64
W_SB = H_SB * D_SB
SP_GROUPS = 8
W_SP = 512
CHUNK = 128
N_MEM = 256
H_MEM = 4
D_MEM = 128
W_MEM = H_MEM * D_MEM
N_BRANCH = 3
D_FF = 2816
CONV_W = 3
EPS = 1e-6
PAGE_SIZE = 128

LOG2E = 1.4426950408889634
MAX_LOGIT2 = 126.0
LANE = 128
SUBLANE = 8
ROW_BLOCK = 256
AUG = 2 * D_SB
FFN_COLS = 256
SB_HEADS = 4
SB_Q_TILES = 4
SB_KEY_TILES = 4
SB_STAGE_LAG = 3
SBS_SLOTS = 3
PAGES_PER_STEP = 8
VMEM_LIMIT = 56 * 1024 * 1024

C_QA = 0
C_KA = C_QA + H_SB * AUG
C_K = C_KA + H_SB * AUG
C_V = C_K + W_SB
C_U = C_V + W_SB
C_VS = C_U + W_SP
C_QM = C_VS + W_SP
C_G = C_QM + W_MEM
C_END = C_G + N_BRANCH * D_MODEL


def _params(n_axes, limit=VMEM_LIMIT):
    return pltpu.CompilerParams(
        dimension_semantics=("arbitrary",) * n_axes, vmem_limit_bytes=limit)


def _resident(shape, index_map):
    return pl.BlockSpec(shape, index_map, pipeline_mode=pl.Buffered(1))


def _rms(x, gain):
    return x * lax.rsqrt(jnp.mean(x * x, axis=-1, keepdims=True) + EPS) * gain


def _gelu(x):
    c = math.sqrt(2.0 / math.pi)
    return x * (0.5 * (1.0 + jnp.tanh(c * (x + 0.044715 * (x * x * x)))))


def _dot(a, b):
    return jnp.dot(a, b, preferred_element_type=F32)


def _softplus2(z):
    return jnp.log2(1.0 + jnp.exp2(z))


def _memkv_kernel(mem_ref, g_ref, w_ref, kv_ref):
    a = _rms(mem_ref[...], g_ref[...]).astype(BF16)
    kv_ref[...] = _dot(a, w_ref[...])


def _memkv(mem, mem_norm, w_mem_kv_bf):
    depth = w_mem_kv_bf.shape[0]
    return pl.pallas_call(
        _memkv_kernel,
        grid=(depth,),
        in_specs=[
            pl.BlockSpec((N_MEM, D_MODEL), lambda l: (0, 0)),
            pl.BlockSpec((None, 1, D_MODEL), lambda l: (l, 0, 0)),
            pl.BlockSpec((None, D_MODEL, 2 * W_MEM), lambda l: (l, 0, 0)),
        ],
        out_specs=pl.BlockSpec((None, N_MEM, 2 * W_MEM), lambda l: (l, 0, 0)),
        out_shape=jax.ShapeDtypeStruct((depth, N_MEM, 2 * W_MEM), F32),
        compiler_params=_params(1),
        name="memkv",
    )(mem, mem_norm.reshape(depth, 1, D_MODEL), w_mem_kv_bf)


def _inproj_kernel(x_ref, g_ref, w_ref, brow_ref, spn_ref,
                   qT_ref, ka_ref, k_ref, v_ref, vT_ref, ug_ref, vn_ref, qm_ref, gate_ref, *, paged):
    a = _rms(x_ref[...], g_ref[...]).astype(BF16)

    def mm(c0, c1):
        return _dot(a, w_ref[:, c0:c1])

    qa = mm(C_QA, C_KA) + brow_ref[:, :H_SB * AUG]
    qT_ref[...] = qa.T.astype(BF16)
    ka = mm(C_KA, C_K) + brow_ref[:, H_SB * AUG:]
    ka_ref[...] = ka.astype(BF16)
    k = mm(C_K, C_V)
    v = mm(C_V, C_U)
    vT = v.T
    vT_ref[...] = vT.astype(BF16)
    if paged:
        kT = k.T
        for j in range(k_ref.shape[0]):
            k_ref[j] = kT[:, j * PAGE_SIZE:(j + 1) * PAGE_SIZE]
            v_ref[j] = vT[:, j * PAGE_SIZE:(j + 1) * PAGE_SIZE]
    else:
        k_ref[...] = k
        v_ref[...] = v
    ug_ref[...] = _gelu(mm(C_U, C_VS)).astype(BF16)
    vn_ref[...] = _rms(_gelu(mm(C_VS, C_QM)), spn_ref[...])
    qm_ref[...] = mm(C_QM, C_G)
    for i in range(N_BRANCH):
        g = mm(C_G + i * D_MODEL, C_G + (i + 1) * D_MODEL)
        gate_ref[:, i * D_MODEL:(i + 1) * D_MODEL] = jax.nn.sigmoid(g)


def _inproj(x, gain, w_all, brow, sp_norm, paged):
    rows = x.shape[0]
    tm = ROW_BLOCK
    nb = rows // tm
    row = lambda i: (i, 0)
    const = lambda i: (0, 0)
    if paged:
        ppb = tm // PAGE_SIZE
        kv_shape = jax.ShapeDtypeStruct((rows // PAGE_SIZE, W_SB, PAGE_SIZE), F32)
        kv_spec = pl.BlockSpec((ppb, W_SB, PAGE_SIZE), lambda i: (i, 0, 0))
    else:
        kv_shape = jax.ShapeDtypeStruct((rows, W_SB), F32)
        kv_spec = pl.BlockSpec((tm, W_SB), row)
    out_shapes = (
        jax.ShapeDtypeStruct((nb, H_SB * AUG, tm), BF16),
        jax.ShapeDtypeStruct((rows, H_SB * AUG), BF16),
        kv_shape,
        kv_shape,
        jax.ShapeDtypeStruct((nb, W_SB, tm), BF16),
        jax.ShapeDtypeStruct((rows, W_SP), BF16),
        jax.ShapeDtypeStruct((rows, W_SP), F32),
        jax.ShapeDtypeStruct((rows, W_MEM), F32),
        jax.ShapeDtypeStruct((rows, N_BRANCH * D_MODEL), F32),
    )
    out_specs = (
        pl.BlockSpec((None, H_SB * AUG, tm), lambda i: (i, 0, 0)),
        pl.BlockSpec((tm, H_SB * AUG), row),
        kv_spec,
        kv_spec,
        pl.BlockSpec((None, W_SB, tm), lambda i: (i, 0, 0)),
        pl.BlockSpec((tm, W_SP), row),
        pl.BlockSpec((tm, W_SP), row),
        pl.BlockSpec((tm, W_MEM), row),
        pl.BlockSpec((tm, N_BRANCH * D_MODEL), row),
    )
    return pl.pallas_call(
        functools.partial(_inproj_kernel, paged=paged),
        grid=(nb,),
        in_specs=[
            pl.BlockSpec((tm, D_MODEL), row),
            _resident((1, D_MODEL), const),
            _resident((D_MODEL, C_END), const),
            _resident((1, 2 * H_SB * AUG), const),
            _resident((1, W_SP), const),
        ],
        out_specs=out_specs,
        out_shape=out_shapes,
        compiler_params=_params(1),
        name="inproj",
    )(x, gain, w_all, brow, sp_norm)


def _sb_chains(chains, upper, zbuf):
    depth = max(len(tiles) for _, tiles in chains)
    order = [(c, p) for p in range(depth) for c, (_, tiles) in enumerate(chains) if p < len(tiles)]
    carries = [carry for carry, _ in chains]
    totals = [None] * len(chains)
    n_slots = zbuf.shape[0]
    assert n_slots == 2 * SB_STAGE_LAG + 1
    mids = {}
    for step in range(len(order) + 2 * SB_STAGE_LAG):
        if step < len(order):
            c, p = order[step]
            k, q, _, _ = chains[c][1][p]
            zbuf[step % n_slots] = jnp.minimum(_dot(k, q), MAX_LOGIT2)
        if 0 <= step - SB_STAGE_LAG < len(order):
            c, p = order[step - SB_STAGE_LAG]
            valid = chains[c][1][p][3]
            sp = _softplus2(zbuf[(step - SB_STAGE_LAG) % n_slots])
            if valid is not None:
                sp = jnp.where(valid, sp, 0.0)
            mids[c, p] = _dot(upper, sp.astype(BF16))
        if 0 <= step - 2 * SB_STAGE_LAG < len(order):
            c, p = order[step - 2 * SB_STAGE_LAG]
            _, _, vT, valid = chains[c][1][p]
            z = zbuf[(step - 2 * SB_STAGE_LAG) % n_slots]
            newer = mids.pop((c, p))
            a = jnp.exp2((z - newer - carries[c]).astype(BF16))
            if valid is not None:
                a = jnp.where(valid, a, jnp.zeros_like(a))
            o = _dot(vT, a)
            totals[c] = o if totals[c] is None else totals[c] + o
            carries[c] = carries[c] + newer[0:1, :]
    return list(zip(totals, carries))


def _sbp_kernel(qT_ref, ka_ref, vT_ref, o_ref, acc_ref, zbuf):
    t = ROW_BLOCK
    qi = pl.program_id(1)
    rk = lax.broadcasted_iota(jnp.int32, (t, t), 0)
    cq = lax.broadcasted_iota(jnp.int32, (t, t), 1)
    upper = jnp.where(cq >= rk, 1.0, 0.0).astype(BF16)
    causal = rk < cq
    zero = jnp.zeros((1, t), F32)
    streams = [(hh, part) for hh in range(SB_HEADS) for part in range(SB_Q_TILES)]

    def tile(kb, hh, part, valid):
        r0 = pl.multiple_of(kb * t, t)
        return (ka_ref[pl.ds(r0, t), hh * AUG:(hh + 1) * AUG],
                qT_ref[part, hh * AUG:(hh + 1) * AUG, :],
                vT_ref[kb, hh * D_SB:(hh + 1) * D_SB, :], valid)

    def acc_at(hh, part):
        return (slice(hh * D_SB, (hh + 1) * D_SB), slice(part * t, (part + 1) * t))

    lo = SB_Q_TILES * qi
    chains = [(zero, [tile(lo + part, hh, part, causal)]
               + [tile(lo + part - 1 - d, hh, part, None) for d in range(part)])
              for hh, part in streams]
    carries = []
    for (hh, part), (o, carry) in zip(streams, _sb_chains(chains, upper, zbuf)):
        acc_ref[acc_at(hh, part)] = o
        carries.append(carry)

    def body(jj, carries):
        kb0 = lo - 1 - jj * SB_KEY_TILES
        chains = [(carries[i], [tile(kb0 - d, hh, part, None) for d in range(SB_KEY_TILES)])
                  for i, (hh, part) in enumerate(streams)]
        out = []
        for (hh, part), (o, carry) in zip(streams, _sb_chains(chains, upper, zbuf)):
            acc_ref[acc_at(hh, part)] += o
            out.append(carry)
        return tuple(out)

    lax.fori_loop(0, lo // SB_KEY_TILES, body, tuple(carries))
    o_ref[...] = acc_ref[...].T


def _sb_prompt(qT, ka, vT):
    nb = qT.shape[0]
    t = ROW_BLOCK
    rows = nb * t
    return pl.pallas_call(
        _sbp_kernel,
        grid=(H_SB // SB_HEADS, nb // SB_Q_TILES),
        in_specs=[
            pl.BlockSpec((SB_Q_TILES, SB_HEADS * AUG, t), lambda hg, qi: (qi, hg, 0)),
            _resident((rows, SB_HEADS * AUG), lambda hg, qi: (0, hg)),
            _resident((nb, SB_HEADS * D_SB, t), lambda hg, qi: (0, hg, 0)),
        ],
        out_specs=pl.BlockSpec((SB_Q_TILES * t, SB_HEADS * D_SB), lambda hg, qi: (qi, hg)),
        out_shape=jax.ShapeDtypeStruct((rows, W_SB), F32),
        scratch_shapes=[pltpu.VMEM((SB_HEADS * D_SB, SB_Q_TILES * t), F32),
                        pltpu.VMEM((2 * SB_STAGE_LAG + 1, t, t), F32)],
        compiler_params=_params(2),
        name="sb_prompt",
    )(qT, ka, vT)


def _sbs_kernel(pt_ref, qbd_ref, bias_ref, kn_ref, vn_ref, ck_hbm, cv_hbm, o_ref,
                kbuf, vbuf, sems, *, layer, n_seq, n_pages):
    g_pages = PAGES_PER_STEP
    n_groups = n_pages // g_pages
    b = pl.program_id(0)
    nq = H_SB * SUBLANE
    qbd = qbd_ref[...]
    bias = bias_ref[...]
    rj = lax.broadcasted_iota(jnp.int32, (PAGE_SIZE, 2 * PAGE_SIZE), 0)
    cs = lax.broadcasted_iota(jnp.int32, (PAGE_SIZE, 2 * PAGE_SIZE), 1)
    newer_tot = jnp.where((rj > cs) | (cs >= PAGE_SIZE), 1.0, 0.0).astype(BF16)
    nt_dims = (((1,), (1,)), ((), ()))

    def group_copies(gidx):
        seq = gidx // n_groups
        group = gidx % n_groups
        slot = gidx % SBS_SLOTS
        copies = []
        for i in range(g_pages):
            page = pt_ref[seq, n_pages - 1 - (group * g_pages + i)]
            lanes = pl.ds(i * PAGE_SIZE, PAGE_SIZE)
            copies.append(pltpu.make_async_copy(
                ck_hbm.at[layer, page], kbuf.at[slot, :, lanes], sems.at[0, slot, i]))
            copies.append(pltpu.make_async_copy(
                cv_hbm.at[layer, page], vbuf.at[slot, :, lanes], sems.at[1, slot, i]))
        return copies

    def start_group(gidx):
        for c in group_copies(gidx):
            c.start()

    def wait_group(gidx):
        for c in group_copies(gidx):
            c.wait()

    def attend(z, carry, valid, av):
        z = jnp.minimum(z, MAX_LOGIT2)
        sp = _softplus2(z)
        log_beta = z - sp
        if valid is not None:
            sp = jnp.where(valid, sp, 0.0)
        spb = sp.astype(BF16)
        n = z.shape[1] // PAGE_SIZE
        nts = [_dot(spb[:, i * PAGE_SIZE:(i + 1) * PAGE_SIZE], newer_tot) for i in range(n)]
        parts = []
        for i, nt in enumerate(nts):
            lanes = slice(i * PAGE_SIZE, (i + 1) * PAGE_SIZE)
            parts.append(jnp.exp2(log_beta[:, lanes] - nt[:, :PAGE_SIZE] - carry))
            carry = carry + nt[:, PAGE_SIZE:]
        a = parts[0] if n == 1 else jnp.concatenate(parts, axis=1)
        if valid is not None:
            a = jnp.where(valid, a, 0.0)
        return av(a.astype(BF16)), carry

    @pl.when(b == 0)
    def _():
        for gidx in range(SBS_SLOTS - 1):
            start_group(gidx)

    pad = jnp.zeros((PAGE_SIZE - SUBLANE, W_SB), F32)
    kn = jnp.concatenate([kn_ref[...], pad], axis=0).astype(BF16)
    vn = jnp.concatenate([vn_ref[...], pad], axis=0).astype(BF16)
    tq = lax.broadcasted_iota(jnp.int32, (nq, PAGE_SIZE), 0) % SUBLANE
    kj = lax.broadcasted_iota(jnp.int32, (nq, PAGE_SIZE), 1)
    z = lax.dot_general(qbd, kn, nt_dims, preferred_element_type=F32) + bias
    acc, carry = attend(z, jnp.zeros((nq, PAGE_SIZE), F32), kj < tq, lambda a: _dot(a, vn))

    bias_wide = jnp.concatenate([bias] * g_pages, axis=1)

    def body(group, state):
        acc, carry = state
        gidx = b * n_groups + group
        slot = gidx % SBS_SLOTS

        @pl.when(gidx + SBS_SLOTS - 1 < n_seq * n_groups)
        def _():
            start_group(gidx + SBS_SLOTS - 1)

        wait_group(gidx)
        kb = kbuf[slot].astype(BF16)
        vb = vbuf[slot].astype(BF16)
        z = _dot(qbd, kb) + bias_wide
        o, carry = attend(z, carry, None,
                          lambda a: lax.dot_general(a, vb, nt_dims, preferred_element_type=F32))
        return acc + o, carry

    acc, _ = lax.fori_loop(0, n_groups, body, (acc, carry))

    rh = lax.broadcasted_iota(jnp.int32, acc.shape, 0) // SUBLANE
    ch = lax.broadcasted_iota(jnp.int32, acc.shape, 1) // D_SB
    own = jnp.where(rh == ch, acc, 0.0)
    out = own[0:SUBLANE, :]
    for h in range(1, H_SB):
        out = out + own[h * SUBLANE:(h + 1) * SUBLANE, :]
    o_ref[...] = out


def _sb_sample(layer, page_table, qbd, bias_full, k_new, v_new, cache_k, cache_v):
    n_seq, n_pages = page_table.shape
    g_pages = PAGES_PER_STEP
    assert n_pages % g_pages == 0 and n_seq * (n_pages // g_pages) >= SBS_SLOTS - 1
    nq = H_SB * SUBLANE
    grid_spec = pltpu.PrefetchScalarGridSpec(
        num_scalar_prefetch=1,
        grid=(n_seq,),
        in_specs=[
            pl.BlockSpec((None, nq, W_SB), lambda b, pt: (b, 0, 0)),
            pl.BlockSpec((nq, PAGE_SIZE), lambda b, pt: (0, 0)),
            pl.BlockSpec((SUBLANE, W_SB), lambda b, pt: (b, 0)),
            pl.BlockSpec((SUBLANE, W_SB), lambda b, pt: (b, 0)),
            pl.BlockSpec(memory_space=pl.ANY),
            pl.BlockSpec(memory_space=pl.ANY),
        ],
        out_specs=pl.BlockSpec((SUBLANE, W_SB), lambda b, pt: (b, 0)),
        scratch_shapes=[
            pltpu.VMEM((SBS_SLOTS, W_SB, g_pages * PAGE_SIZE), F32),
            pltpu.VMEM((SBS_SLOTS, W_SB, g_pages * PAGE_SIZE), F32),
            pltpu.SemaphoreType.DMA((2, SBS_SLOTS, g_pages)),
        ],
    )
    return pl.pallas_call(
        functools.partial(_sbs_kernel, layer=layer, n_seq=n_seq, n_pages=n_pages),
        grid_spec=grid_spec,
        out_shape=jax.ShapeDtypeStruct((n_seq * SUBLANE, W_SB), F32),
        compiler_params=_params(1),
        name="sb_sample",
    )(page_table, qbd, bias_full, k_new, v_new, cache_k, cache_v)


def _memattn_kernel(q_ref, mk_ref, mv_ref, o_ref):
    for h in range(H_MEM):
        cols = slice(h * D_MEM, (h + 1) * D_MEM)
        q = q_ref[:, cols].astype(BF16)
        s = lax.dot_general(q, mk_ref[:, cols].astype(BF16), (((1,), (1,)), ((), ())),
                            preferred_element_type=F32)
        e = jnp.exp(s - jnp.max(s, axis=-1, keepdims=True))
        p = e / jnp.sum(e, axis=-1, keepdims=True)
        o_ref[:, cols] = _dot(p.astype(BF16), mv_ref[:, cols].astype(BF16))


def _mem_attend(q, mk, mv, mk_at, mv_at, n_sets, tq):
    nrb = q.shape[0] // (n_sets * tq)
    return pl.pallas_call(
        _memattn_kernel,
        grid=(n_sets, nrb),
        in_specs=[
            pl.BlockSpec((tq, W_MEM), lambda b, r: (b * nrb + r, 0)),
            pl.BlockSpec((None, N_MEM, W_MEM), lambda b, r: mk_at(b)),
            pl.BlockSpec((None, N_MEM, W_MEM), lambda b, r: mv_at(b)),
        ],
        out_specs=pl.BlockSpec((tq, W_MEM), lambda b, r: (b * nrb + r, 0)),
        out_shape=jax.ShapeDtypeStruct(q.shape, F32),
        compiler_params=_params(2),
        name="mem_attend",
    )(q, mk, mv)


def _mix_kernel(h_ref, osb_ref, omem_ref, ug_ref, vn_ref, gate_ref, wt_ref, bt_ref,
                wbr_ref, wo_ref, out_ref, *, chunk):
    tm = h_ref.shape[0]
    shift = chunk.bit_length() - 1
    r = lax.broadcasted_iota(jnp.int32, (tm, tm), 0)
    c = lax.broadcasted_iota(jnp.int32, (tm, tm), 1)
    causal = (c <= r) & ((r >> shift) == (c >> shift))
    lane = lax.broadcasted_iota(jnp.int32, (tm, LANE), 1)
    gw = W_SP // SP_GROUPS
    vnb = vn_ref[...].astype(BF16)
    tiles = []
    for j in range(W_SP // LANE):
        vt = vnb[:, j * LANE:(j + 1) * LANE]
        per_group = []
        for g in range(LANE // gw):
            wg = wt_ref[j * (LANE // gw) + g]
            w = jnp.where(causal, wg, jnp.zeros_like(wg))
            per_group.append(_dot(w, vt))
        tiles.append(jnp.where(lane < gw, per_group[0], per_group[1]))
    s = jnp.concatenate(tiles, axis=1) + bt_ref[...]
    osp = ug_ref[...].astype(F32) * s

    mixed = None
    for i, o in enumerate((osb_ref[...], osp, omem_ref[...])):
        br = _dot(o.astype(BF16), wbr_ref[i]) * gate_ref[:, i * D_MODEL:(i + 1) * D_MODEL]
        mixed = br if mixed is None else mixed + br
    out_ref[...] = h_ref[...] + _dot(mixed.astype(BF16), wo_ref[...])


def _mix(h, osb, omem, ug, vn, gates, wt, bt, wbr, wo, chunk):
    rows = h.shape[0]
    tm = ROW_BLOCK
    row = lambda i: (i, 0)
    return pl.pallas_call(
        functools.partial(_mix_kernel, chunk=chunk),
        grid=(rows // tm,),
        in_specs=[
            pl.BlockSpec((tm, D_MODEL), row),
            pl.BlockSpec((tm, W_SB), row),
            pl.BlockSpec((tm, W_MEM), row),
            pl.BlockSpec((tm, W_SP), row),
            pl.BlockSpec((tm, W_SP), row),
            pl.BlockSpec((tm, N_BRANCH * D_MODEL), row),
            _resident((SP_GROUPS, tm, tm), lambda i: (0, 0, 0)),
            _resident((tm, W_SP), lambda i: (0, 0)),
            _resident((N_BRANCH, W_SB, D_MODEL), lambda i: (0, 0, 0)),
            _resident((D_MODEL, D_MODEL), lambda i: (0, 0)),
        ],
        out_specs=pl.BlockSpec((tm, D_MODEL), row),
        out_shape=jax.ShapeDtypeStruct((rows, D_MODEL), F32),
        compiler_params=_params(1),
        name="mix",
    )(h, osb, omem, ug, vn, gates, wt, bt, wbr, wo)


def _ffn_kernel(*refs, seq_len, final_norm):
    h_ref, g_ref, wup_ref, cw_ref, cb_ref, wdn_ref, gf_ref = refs[:7]
    if seq_len is None:
        out_ref, up_ref, act_ref, carry_ref = refs[7:]
    else:
        s1_ref, s2_ref = refs[7:9]
        out_ref, up_ref, act_ref = refs[9:]
    tm = h_ref.shape[0]
    i = pl.program_id(0)
    h = h_ref[...]
    f = _rms(h, g_ref[...]).astype(BF16)
    pos = lax.broadcasted_iota(jnp.int32, (tm, FFN_COLS), 0)
    if seq_len is None:
        @pl.when(i == 0)
        def _():
            carry_ref[...] = jnp.zeros_like(carry_ref)
    else:
        pos = pos % seq_len

    for c in range(D_FF // FFN_COLS):
        conv = []
        for part in range(2):
            cols = slice(part * D_FF + c * FFN_COLS, part * D_FF + (c + 1) * FFN_COLS)
            up = _dot(f, wup_ref[:, cols])
            if seq_len is None:
                prev = carry_ref[:, cols]
                p1 = jnp.broadcast_to(prev[SUBLANE - 1:SUBLANE, :], up.shape)
                p2 = jnp.where(pos == 0, prev[SUBLANE - 2:SUBLANE - 1, :], p1)
                carry_ref[:, cols] = up[tm - SUBLANE:, :]
                up_ref[:, cols] = up[tm - SUBLANE:, :]
            else:
                p1 = s1_ref[:, cols]
                p2 = s2_ref[:, cols]
                up_ref[:, cols] = up
            x1 = jnp.where(pos >= 1, pltpu.roll(up, 1, 0), p1)
            x2 = jnp.where(pos >= 2, pltpu.roll(up, 2, 0), p2)
            conv.append(cb_ref[:, cols] + cw_ref[0:1, cols] * x2 + cw_ref[1:2, cols] * x1
                        + cw_ref[2:3, cols] * up)
        act_ref[:, c * FFN_COLS:(c + 1) * FFN_COLS] = (_gelu(conv[0]) * conv[1]).astype(BF16)

    y = h + _dot(act_ref[...], wdn_ref[...])
    if final_norm:
        y = _rms(y, gf_ref[...])
    out_ref[...] = y


def _ffn(h, gain, wup, cw, cb, wdn, gfinal, prev_rows, seq_len, final_norm):
    rows = h.shape[0]
    tm = ROW_BLOCK
    row = lambda i: (i, 0)
    const = lambda i: (0, 0)
    in_specs = [
        pl.BlockSpec((tm, D_MODEL), row),
        _resident((1, D_MODEL), const),
        _resident((D_MODEL, 2 * D_FF), const),
        _resident((CONV_W, 2 * D_FF), const),
        _resident((1, 2 * D_FF), const),
        _resident((D_FF, D_MODEL), const),
        _resident((1, D_MODEL), const),
    ]
    args = [h, gain, wup, cw, cb, wdn, gfinal]
    scratch = [pltpu.VMEM((tm, D_FF), BF16)]
    if seq_len is None:
        up_rows = SUBLANE
        up_spec = pl.BlockSpec((SUBLANE, 2 * D_FF), const)
        scratch.append(pltpu.VMEM((SUBLANE, 2 * D_FF), F32))
    else:
        up_rows = rows
        up_spec = pl.BlockSpec((tm, 2 * D_FF), row)
        in_specs += [_resident((tm, 2 * D_FF), row)] * 2
        args += list(prev_rows)
    return pl.pallas_call(
        functools.partial(_ffn_kernel, seq_len=seq_len, final_norm=final_norm),
        grid=(rows // tm,),
        in_specs=in_specs,
        out_specs=(pl.BlockSpec((tm, D_MODEL), row), up_spec),
        out_shape=(jax.ShapeDtypeStruct((rows, D_MODEL), F32),
                   jax.ShapeDtypeStruct((up_rows, 2 * D_FF), F32)),
        scratch_shapes=scratch,
        compiler_params=_params(1),
        name="ffn",
    )(*args)


def _augment(w, scale):
    w = (w * scale).reshape(D_MODEL, H_SB, D_SB)
    return jnp.pad(w, ((0, 0), (0, 0), (0, AUG - D_SB))).reshape(D_MODEL, H_SB * AUG)


def _fused_in_weight(w_in_l):
    sizes = [W_SB, W_SB, W_SB, W_SP, W_SP, W_MEM, N_BRANCH * D_MODEL]
    offs = [0]
    for s in sizes:
        offs.append(offs[-1] + s)
    wq, wk, wv, wu, wvs, wqm, wg = [w_in_l[:, offs[i]:offs[i + 1]] for i in range(7)]
    parts = [_augment(wq, LOG2E * D_SB ** -0.5), _augment(wk, 1.0), wk, wv, wu, wvs,
             wqm * D_MEM ** -0.5, wg]
    return jnp.concatenate(parts, axis=1).astype(BF16)


def _bias_row(sb_bias_l):
    b2 = sb_bias_l.astype(F32) * LOG2E
    hi = b2.astype(BF16).astype(F32)
    lo = (b2 - hi).astype(BF16).astype(F32)
    zq = jnp.zeros((H_SB, AUG), F32).at[:, D_SB].set(1.0).at[:, D_SB + 1].set(1.0)
    zk = jnp.zeros((H_SB, AUG), F32).at[:, D_SB].set(hi).at[:, D_SB + 1].set(lo)
    return jnp.concatenate([zq.reshape(1, -1), zk.reshape(1, -1)], axis=1)


def _spatial_operands(sp_w_l, sp_b_l, chunk, tm):
    reps = tm // chunk
    wt = jnp.tile(sp_w_l[:, :chunk, :chunk].astype(BF16), (1, reps, reps))
    bt = jnp.tile(jnp.repeat(sp_b_l[:, :chunk].T, W_SP // SP_GROUPS, axis=1), (reps, 1))
    return wt, bt


def kernel(x_prompt, x_sample, mem_prompt, cache_sb_k, cache_sb_v, page_table, cache_mem_k,
           cache_mem_v, state_ffn_conv, norm_mix, w_in, sb_bias, sp_norm, sp_w, sp_b, mem_norm,
           w_mem_kv, w_branch, w_o, norm_ffn, w_up, conv_w, conv_b, w_down, norm_final):
    depth = w_in.shape[0]
    batch, seq, _ = x_prompt.shape
    n_seq, dec_seq, _ = x_sample.shape
    assert batch == 1 and dec_seq == SUBLANE and n_seq * dec_seq == ROW_BLOCK
    assert seq % (SB_Q_TILES * ROW_BLOCK) == 0 and page_table.shape[1] % PAGES_PER_STEP == 0

    hp = x_prompt.reshape(seq, D_MODEL)
    hs = x_sample.reshape(n_seq * dec_seq, D_MODEL)
    n_pool = cache_sb_k.shape[1]
    cache_k = jnp.transpose(cache_sb_k, (0, 1, 3, 4, 2)).reshape(depth, n_pool, W_SB, PAGE_SIZE)
    cache_v = jnp.transpose(cache_sb_v, (0, 1, 3, 4, 2)).reshape(depth, n_pool, W_SB, PAGE_SIZE)
    mem_kv = _memkv(mem_prompt.reshape(N_MEM, D_MODEL), mem_norm, w_mem_kv.astype(BF16))
    mem_k_s = cache_mem_k.reshape(depth * n_seq, N_MEM, W_MEM)
    mem_v_s = cache_mem_v.reshape(depth * n_seq, N_MEM, W_MEM)
    gfinal = norm_final.reshape(1, D_MODEL)
    head_eye = jnp.eye(H_SB, dtype=BF16)

    pk, pv, sk, sv, spv, pconv, sconv = [], [], [], [], [], [], []
    for l in range(depth):
        w_all = _fused_in_weight(w_in[l])
        brow = _bias_row(sb_bias[l])
        gain_mix = norm_mix[l].reshape(1, D_MODEL)
        spn = sp_norm[l].reshape(1, W_SP)
        wbr = w_branch[l].astype(BF16)
        wo = w_o[l].astype(BF16)
        gain_ffn = norm_ffn[l].reshape(1, D_MODEL)
        wup = w_up[l].astype(BF16)
        wdn = w_down[l].astype(BF16)
        cw = conv_w[l]
        cb = conv_b[l].reshape(1, 2 * D_FF)
        last = l == depth - 1

        qT, ka, k_p, v_p, vT, ug, vn, qm, gates = _inproj(hp, gain_mix, w_all, brow, spn, True)
        osb = _sb_prompt(qT, ka, vT)
        omem = _mem_attend(qm, mem_kv, mem_kv, lambda b, l=l: (l, 0, 0), lambda b, l=l: (l, 0, 1),
                           1, 2 * ROW_BLOCK)
        wt, bt = _spatial_operands(sp_w[l], sp_b[l], CHUNK, ROW_BLOCK)
        hp = _mix(hp, osb, omem, ug, vn, gates, wt, bt, wbr, wo, CHUNK)
        hp, up_last = _ffn(hp, gain_ffn, wup, cw, cb, wdn, gfinal, None, None, last)
        pk.append(k_p)
        pv.append(v_p)
        pconv.append(up_last[SUBLANE - (CONV_W - 1):, :])

        qT, ka, k_s, v_s, vT, ug, vn, qm, gates = _inproj(hs, gain_mix, w_all, brow, spn, False)
        q = qT.reshape(H_SB, AUG, n_seq, dec_seq)[:, :D_SB]
        qbd = jnp.einsum('hdbt,hg->bhtgd', q, head_eye).reshape(n_seq, H_SB * dec_seq, W_SB)
        bias_full = jnp.broadcast_to(
            jnp.repeat(sb_bias[l].astype(F32) * LOG2E, dec_seq)[:, None], (H_SB * dec_seq, PAGE_SIZE))
        osb = _sb_sample(l, page_table, qbd, bias_full, k_s, v_s, cache_k, cache_v)
        at = lambda b, l=l: (l * n_seq + b, 0, 0)
        omem = _mem_attend(qm, mem_k_s, mem_v_s, at, at, n_seq, dec_seq)
        wt, bt = _spatial_operands(sp_w[l], sp_b[l], dec_seq, ROW_BLOCK)
        hs = _mix(hs, osb, omem, ug, vn, gates, wt, bt, wbr, wo, dec_seq)
        st = state_ffn_conv[l]
        s1 = jnp.repeat(st[:, 1:2, :], dec_seq, axis=1).reshape(n_seq * dec_seq, 2 * D_FF)
        s2 = jnp.tile(st, (1, dec_seq // (CONV_W - 1), 1)).reshape(n_seq * dec_seq, 2 * D_FF)
        hs, up_s = _ffn(hs, gain_ffn, wup, cw, cb, wdn, gfinal, (s1, s2), dec_seq, last)
        sk.append(k_s)
        sv.append(v_s)
        spv.append(vn)
        sconv.append(up_s.reshape(n_seq, dec_seq, 2 * D_FF)[:, dec_seq - (CONV_W - 1):, :])

    n_pp = seq // PAGE_SIZE
    y_prompt = hp.reshape(batch, seq, D_MODEL)
    y_sample = hs.reshape(n_seq, dec_seq, D_MODEL)
    return (
        y_prompt,
        y_sample,
        jnp.transpose(jnp.stack(pk).reshape(depth, n_pp, H_SB, D_SB, PAGE_SIZE), (0, 1, 4, 2, 3)),
        jnp.transpose(jnp.stack(pv).reshape(depth, n_pp, H_SB, D_SB, PAGE_SIZE), (0, 1, 4, 2, 3)),
        jnp.stack(sk).reshape(depth, n_seq, dec_seq, H_SB, D_SB),
        jnp.stack(sv).reshape(depth, n_seq, dec_seq, H_SB, D_SB),
        mem_kv[:, :, :W_MEM].reshape(depth, batch, N_MEM, H_MEM, D_MEM),
        mem_kv[:, :, W_MEM:].reshape(depth, batch, N_MEM, H_MEM, D_MEM),
        jnp.stack(spv).reshape(depth, n_seq, dec_seq, W_SP),
        jnp.stack(pconv).reshape(depth, batch, CONV_W - 1, 2 * D_FF),
        jnp.stack(sconv).reshape(depth, n_seq, CONV_W - 1, 2 * D_FF),
    )
```
